```python
import functools
import jax
import jax.numpy as jnp
from jax import lax
import numpy as np

D_MODEL = 2048
BATCH = 8
SEQ = 2048
DEPTH = 2

CTX_LEN = 256
GRID_W = 64
N_EVEN = (DEPTH + 1) // 2
N_ODD = DEPTH // 2
NORM_EPS = 1e-6
F32 = jnp.float32

ATT_HEADS = 8
ATT_KV_HEADS = 2
ATT_GROUP = ATT_HEADS // ATT_KV_HEADS
HEAD_DIM = 128
ATT_WIDTH = ATT_HEADS * HEAD_DIM
KV_WIDTH = ATT_KV_HEADS * HEAD_DIM
ROPE_BASE = 10000.0
Q_BLOCK = 128
CONV_WIDTH = D_MODEL // 2
CONV_KERNEL = 31
AB_IN = ATT_WIDTH + 2 * KV_WIDTH + 2 * CONV_WIDTH
AB_OUT = ATT_WIDTH + CONV_WIDTH
AB_SPLITS = [ATT_WIDTH, ATT_WIDTH + KV_WIDTH, ATT_WIDTH + 2 * KV_WIDTH]
RWKV_HEADS = 16
RWKV_HEAD_DIM = 64
RWKV_WIDTH = RWKV_HEADS * RWKV_HEAD_DIM
DECAY_LORA = 64
AAA_LORA = 64
GATE_LORA = 160
RWKV_IN = 3 * RWKV_WIDTH + DECAY_LORA + AAA_LORA + GATE_LORA
RWKV_SPLITS = [RWKV_WIDTH, 2 * RWKV_WIDTH, 3 * RWKV_WIDTH, 3 * RWKV_WIDTH + DECAY_LORA,
               3 * RWKV_WIDTH + DECAY_LORA + AAA_LORA]
RWKV_GN_EPS = 64e-5
RET_HEADS = 8
RET_QK_DIM = 128
RET_V_DIM = 256
RET_QK_WIDTH = RET_HEADS * RET_QK_DIM
RET_V_WIDTH = RET_HEADS * RET_V_DIM
RET_CHUNK = 128
CD_IN = RWKV_IN + 2 * RET_QK_WIDTH + 2 * RET_V_WIDTH
CD_OUT = RWKV_WIDTH + RET_V_WIDTH
CD_SPLITS = [RWKV_IN, RWKV_IN + RET_QK_WIDTH, RWKV_IN + 2 * RET_QK_WIDTH,
             RWKV_IN + 2 * RET_QK_WIDTH + RET_V_WIDTH]
FFN_DIM = 5632
FFN_CONV = 3

kernel_name = 'hybrid_diffusion_trunk'


def rms_norm(x, g):
    xf = x.astype(F32)
    y = xf * lax.rsqrt(jnp.mean(xf * xf, axis=-1, keepdims=True) + NORM_EPS)
    return (y * g.astype(F32)).astype(x.dtype)


def layer_norm(x, g, b, eps):
    xf = x.astype(F32)
    xc = xf - jnp.mean(xf, axis=-1, keepdims=True)
    var = jnp.mean(xc * xc, axis=-1, keepdims=True)
    return (xc * lax.rsqrt(var + eps) * g.astype(F32) + b.astype(F32)).astype(x.dtype)


def head_group_norm(x, g, b, eps):
    B, T, H, N = x.shape
    return layer_norm(x, g.reshape(H, N), b.reshape(H, N), eps).reshape(B, T, H * N)


def modulate(x, shift, scale):
    return x * (1 + scale) + shift


def dwconv(x, w, b):
    K = w.shape[0]
    y = lax.conv_general_dilated(x, w[:, None, :].astype(x.dtype), window_strides=(1,),
                                 padding=[(K // 2, K // 2)],
                                 dimension_numbers=('NWC', 'WIO', 'NWC'),
                                 feature_group_count=x.shape[-1])
    return y + b.astype(x.dtype)


def token_shift(z, mu):
    z_prev = jnp.pad(z, ((0, 0), (1, 0), (0, 0)))[:, :-1]
    z_next = jnp.pad(z, ((0, 0), (0, 1), (0, 0)))[:, 1:]
    return z + mu[0] * (z_prev - z) + mu[1] * (z_next - z)


def axial_rope_tables(n_tok, head_dim):
    rows = n_tok // GRID_W
    row = jnp.repeat(jnp.arange(rows, dtype=F32), GRID_W)
    col = jnp.tile(jnp.arange(GRID_W, dtype=F32), rows)
    axis_dim = head_dim // 2
    inv_freq = ROPE_BASE ** (-jnp.arange(0, axis_dim, 2, dtype=F32) / axis_dim)
    ang = jnp.concatenate([row[:, None] * inv_freq, col[:, None] * inv_freq], axis=-1)
    return jnp.cos(ang), jnp.sin(ang)


def apply_axial_rope(x, cos, sin):
    B, T, H, d = x.shape
    q = d // 4
    xf = x.astype(F32).reshape(B, T, H, 2, 2 * q)
    x1, x2 = xf[..., :q], xf[..., q:]
    c = cos.reshape(T, 2, q)[None, :, None]
    s = sin.reshape(T, 2, q)[None, :, None]
    out = jnp.concatenate([x1 * c - x2 * s, x1 * s + x2 * c], axis=-1)
    return out.reshape(B, T, H, d).astype(x.dtype)


def block_attention(q, k, v):
    B, T, Hk, G, d = q.shape
    nb = T // Q_BLOCK
    qb = jnp.moveaxis(q.reshape(B, nb, Q_BLOCK, Hk, G, d), 1, 0)
    scale = d ** -0.5

    def one_block(qi):
        s = jnp.einsum('bqhgd,bnhd->bhgqn', qi, k, preferred_element_type=F32) * scale
        p = jax.nn.softmax(s, axis=-1).astype(v.dtype)
        return jnp.einsum('bhgqn,bnhd->bqhgd', p, v)

    o = lax.map(one_block, qb)
    return jnp.moveaxis(o, 0, 1).reshape(B, T, Hk, G, d)


def conv_ffn(u, w_up, conv_w, conv_b, w_down):
    gate, val = jnp.split(u @ w_up, 2, axis=-1)
    gate = dwconv(gate, conv_w, conv_b)
    return (jax.nn.silu(gate) * val) @ w_down


def mixer_ab(u_lat, u_ctx, w_in, q_norm_g, k_norm_g, conv_w, conv_b, conv_norm_g, conv_norm_b,
             w_out, need_ctx):
    def project(u):
        B, T, _ = u.shape
        q, k, v, glu = jnp.split(u @ w_in, AB_SPLITS, axis=-1)
        q = rms_norm(q.reshape(B, T, ATT_HEADS, HEAD_DIM), q_norm_g)
        k = rms_norm(k.reshape(B, T, ATT_KV_HEADS, HEAD_DIM), k_norm_g)
        return q, k, v.reshape(B, T, ATT_KV_HEADS, HEAD_DIM), glu

    def conformer_conv(glu):
        a, b = jnp.split(glu, 2, axis=-1)
        h = dwconv(a * jax.nn.sigmoid(b), conv_w, conv_b)
        return jax.nn.silu(layer_norm(h, conv_norm_g, conv_norm_b, NORM_EPS))

    def attend(q, k, v):
        B, T = q.shape[:2]
        q = q.reshape(B, T, ATT_KV_HEADS, ATT_GROUP, HEAD_DIM)
        return block_attention(q, k, v).reshape(B, T, ATT_WIDTH)

    q_c, k_c, v_c, glu_c = project(u_ctx)
    q_l, k_l, v_l, glu_l = project(u_lat)
    cos, sin = axial_rope_tables(u_lat.shape[1], HEAD_DIM)
    q_l = apply_axial_rope(q_l, cos, sin)
    k_l = apply_axial_rope(k_l, cos, sin)
    att_l = attend(q_l, jnp.concatenate([k_l, k_c], axis=1), jnp.concatenate([v_l, v_c], axis=1))
    y_l = jnp.concatenate([att_l, conformer_conv(glu_l)], axis=-1) @ w_out
    if not need_ctx:
        return y_l, None
    att_c = attend(q_c, k_c, v_c)
    y_c = jnp.concatenate([att_c, conformer_conv(glu_c)], axis=-1) @ w_out
    return y_l, y_c


def rwkv7_scan(S0, r, w, k, v, kk, a):
    def step(S, inp):
        r_t, w_t, k_t, v_t, kk_t, a_t = inp
        sa = jnp.einsum('bhij,bhj->bhi', S, -kk_t)
        S = (S * w_t[:, :, None, :] + sa[..., None] * (kk_t * a_t)[:, :, None, :]
             + v_t[..., None] * k_t[:, :, None, :])
        return S, jnp.einsum('bhij,bhj->bhi', S, r_t)

    xs = tuple(jnp.moveaxis(t, 1, 0) for t in (r, w, k, v, kk, a))
    S, y = lax.scan(step, S0, xs)
    return jnp.moveaxis(y, 0, 1), S


def retention_chunks(R0, q, k, v, log_gamma):
    B, T, H, dk = q.shape
    dv = v.shape[-1]
    n = T // RET_CHUNK
    idx = jnp.arange(RET_CHUNK, dtype=F32)
    lg = log_gamma.astype(F32)[:, None]
    diff = idx[:, None] - idx[None, :]
    inner_decay = jnp.where(diff >= 0, jnp.exp(lg[:, :, None] * jnp.maximum(diff, 0.0)), 0.0)
    q_decay = jnp.exp(lg * (idx + 1.0))[:, :, None]
    k_decay = jnp.exp(lg * (RET_CHUNK - 1.0 - idx))[:, :, None]
    chunk_decay = jnp.exp(lg * RET_CHUNK)[:, :, None]

    def chunks(t):
        return jnp.moveaxis(t.astype(F32).reshape(B, n, RET_CHUNK, H, t.shape[-1]), (1, 3), (0, 2))

    def step(R, inp):
        qc, kc, vc = inp
        s = jnp.einsum('bhid,bhjd->bhij', qc, kc) * inner_decay
        o = jnp.einsum('bhij,bhjv->bhiv', s, vc) + jnp.einsum('bhid,bhdv->bhiv', qc, R) * q_decay
        R = R * chunk_decay + jnp.einsum('bhjd,bhjv->bhdv', kc * k_decay, vc)
        return R, o

    R, o = lax.scan(step, R0, (chunks(q), chunks(k), chunks(v)))
    return jnp.moveaxis(o, (0, 2), (1, 3)).reshape(B, T, H, dv), R


def bidirectional(scan_f, scan_b, ctx_f, lat_f, ctx_b, lat_b, state0):
    flip = lambda ts: tuple(jnp.flip(t, axis=1) for t in ts)
    yc_f, sc_f = scan_f(state0, *ctx_f)
    yl_f, _ = scan_f(sc_f, *lat_f)
    yc_b, sc_b = scan_b(state0, *flip(ctx_b))
    yl_b, _ = scan_b(sc_b, *flip(lat_b))
    return yl_f + jnp.flip(yl_b, axis=1), yc_f + jnp.flip(yc_b, axis=1)


def mixer_cd(u_lat, u_ctx, w_in, shift_mu, w0, w2, a0, a2, g2, k_k, k_a, r_k, ln_g, ln_b,
             decay_logit, gn_g, gn_b, w_out, need_ctx):
    def features(u, rotary):
        B, T, _ = u.shape
        z, rq, rk, rv, rg = jnp.split(u @ w_in, CD_SPLITS, axis=-1)
        r, k, v, dw, da, dg = jnp.split(token_shift(z, shift_mu), RWKV_SPLITS, axis=-1)
        hd = lambda t: t.astype(F32).reshape(B, T, RWKV_HEADS, RWKV_HEAD_DIM)
        kk = hd(k * k_k)
        kk = kk * lax.rsqrt(jnp.sum(kk * kk, axis=-1, keepdims=True) + 1e-12)
        dirs = []
        for d in range(2):
            wl = -jax.nn.softplus(-(w0[d] + jnp.tanh(dw) @ w2[d])) - 0.5
            decay = jnp.exp(-jnp.exp(wl.astype(F32)))
            a = jax.nn.sigmoid(a0[d] + da @ a2[d])
            kd = k * (1 + (a - 1) * k_a)
            dirs.append((hd(r), hd(decay), hd(kd), hd(v), kk, hd(a)))
        gate = jax.nn.sigmoid(dg) @ g2
        q_r = rq.reshape(B, T, RET_HEADS, RET_QK_DIM)
        k_r = rk.reshape(B, T, RET_HEADS, RET_QK_DIM)
        if rotary:
            cos, sin = axial_rope_tables(T, RET_QK_DIM)
            q_r = apply_axial_rope(q_r, cos, sin)
            k_r = apply_axial_rope(k_r, cos, sin)
        k_r = k_r * RET_QK_DIM ** -0.5
        ret = (q_r, k_r, rv.reshape(B, T, RET_HEADS, RET_V_DIM))
        return dirs, gate, ret, rg

    c_dirs, c_gate, c_ret, c_rg = features(u_ctx, False)
    l_dirs, l_gate, l_ret, l_rg = features(u_lat, True)
    B = u_lat.shape[0]
    S0 = jnp.zeros((B, RWKV_HEADS, RWKV_HEAD_DIM, RWKV_HEAD_DIM), F32)
    wkv_l, wkv_c = bidirectional(rwkv7_scan, rwkv7_scan, c_dirs[0], l_dirs[0], c_dirs[1], l_dirs[1], S0)
    lg = jax.nn.log_sigmoid(decay_logit.astype(F32))
    R0 = jnp.zeros((B, RET_HEADS, RET_QK_DIM, RET_V_DIM), F32)
    ret_l, ret_c = bidirectional(functools.partial(retention_chunks, log_gamma=lg[0]),
                                 functools.partial(retention_chunks, log_gamma=lg[1]),
                                 c_ret, l_ret, c_ret, l_ret, R0)

    def output(wkv, dirs, gate, ret, rg, dtype):
        Bq, T = wkv.shape[:2]
        bonus = sum(jnp.sum(r * kd * r_k.astype(F32), axis=-1, keepdims=True) * v
                    for (r, _, kd, v, _, _) in dirs)
        y_c = (head_group_norm(wkv, ln_g, ln_b, RWKV_GN_EPS) + bonus.reshape(Bq, T, RWKV_WIDTH)) * gate
        y_d = head_group_norm(ret, gn_g, gn_b, NORM_EPS) * jax.nn.silu(rg)
        return jnp.concatenate([y_c.astype(dtype), y_d.astype(dtype)], axis=-1) @ w_out

    y_l = output(wkv_l, l_dirs, l_gate, ret_l, l_rg, u_lat.dtype)
    if not need_ctx:
        return y_l, None
    return y_l, output(wkv_c, c_dirs, c_gate, ret_c, c_rg, u_ctx.dtype)


def setup_inputs(seed: int = 0) -> dict:
    key = jax.random.key(seed)
    ks = iter(jax.random.split(key, 48))
    nrm = lambda shape, scale: jax.random.normal(next(ks), shape, F32) * scale
    uni = lambda shape, lo, hi: jax.random.uniform(next(ks), shape, F32, lo, hi)
    D = D_MODEL
    base = 1.0 - 2.0 ** (-5.0 - np.arange(RET_HEADS, dtype=np.float32))
    base_logit = jnp.asarray(np.log(base) - np.log1p(-base), F32)
    return {
        'x': nrm((BATCH, SEQ, D), 1.0),
        'c': nrm((BATCH, D), 1.0),
        'ctx': nrm((BATCH, CTX_LEN, D), 1.0),
        'c_ctx': nrm((D,), 1.0),
        'mod_w': nrm((DEPTH, D, 6 * D), 0.5 * D ** -0.5),
        'mod_b': nrm((DEPTH, 6 * D), 0.02),
        'norm_mix_g': 1.0 + nrm((DEPTH, D), 0.02),
        'norm_ffn_g': 1.0 + nrm((DEPTH, D), 0.02),
        'ffn_w_up': nrm((DEPTH, D, 2 * FFN_DIM), D ** -0.5),
        'ffn_conv_w': nrm((DEPTH, FFN_CONV, FFN_DIM), FFN_CONV ** -0.5),
        'ffn_conv_b': nrm((DEPTH, FFN_DIM), 0.02),
        'ffn_w_down': nrm((DEPTH, FFN_DIM, D), FFN_DIM ** -0.5),
        'ab_w_in': nrm((N_EVEN, D, AB_IN), D ** -0.5),
        'ab_q_norm': 1.0 + nrm((N_EVEN, HEAD_DIM), 0.02),
        'ab_k_norm': 1.0 + nrm((N_EVEN, HEAD_DIM), 0.02),
        'ab_conv_w': nrm((N_EVEN, CONV_KERNEL, CONV_WIDTH), CONV_KERNEL ** -0.5),
        'ab_conv_b': nrm((N_EVEN, CONV_WIDTH), 0.02),
        'ab_conv_norm_g': 1.0 + nrm((N_EVEN, CONV_WIDTH), 0.02),
        'ab_conv_norm_b': nrm((N_EVEN, CONV_WIDTH), 0.02),
        'ab_w_out': nrm((N_EVEN, AB_OUT, D), AB_OUT ** -0.5),
        'cd_w_in': nrm((N_ODD, D, CD_IN), D ** -0.5),
        'cd_shift_mu': uni((N_ODD, 2, RWKV_IN), 0.0, 0.5),
        'rwkv_w0': uni((N_ODD, 2, RWKV_WIDTH), -4.0, 1.0),
        'rwkv_w2': nrm((N_ODD, 2, DECAY_LORA, RWKV_WIDTH), 0.5 * DECAY_LORA ** -0.5),
        'rwkv_a0': nrm((N_ODD, 2, RWKV_WIDTH), 0.1),
        'rwkv_a2': nrm((N_ODD, 2, AAA_LORA, RWKV_WIDTH), 0.5 * AAA_LORA ** -0.5),
        'rwkv_g2': nrm((N_ODD, GATE_LORA, RWKV_WIDTH), GATE_LORA ** -0.5),
        'rwkv_k_k': 0.85 + nrm((N_ODD, RWKV_WIDTH), 0.05),
        'rwkv_k_a': 1.0 + nrm((N_ODD, RWKV_WIDTH), 0.05),
        'rwkv_r_k': nrm((N_ODD, RWKV_HEADS, RWKV_HEAD_DIM), 0.1),
        'rwkv_ln_g': 1.0 + nrm((N_ODD, RWKV_WIDTH), 0.02),
        'rwkv_ln_b': nrm((N_ODD, RWKV_WIDTH), 0.02),
        'ret_decay_logit': base_logit + nrm((N_ODD, 2, RET_HEADS), 0.1),
        'ret_gn_g': 1.0 + nrm((N_ODD, RET_V_WIDTH), 0.02),
        'ret_gn_b': nrm((N_ODD, RET_V_WIDTH), 0.02),
        'cd_w_out': nrm((N_ODD, CD_OUT, D), CD_OUT ** -0.5),
        'final_norm_g': 1.0 + nrm((D,), 0.02),
    }


def reference(x, c, ctx, c_ctx, mod_w, mod_b, norm_mix_g, norm_ffn_g, ffn_w_up, ffn_conv_w,
              ffn_conv_b, ffn_w_down, ab_w_in, ab_q_norm, ab_k_norm, ab_conv_w, ab_conv_b,
              ab_conv_norm_g, ab_conv_norm_b, ab_w_out, cd_w_in, cd_shift_mu, rwkv_w0, rwkv_w2,
              rwkv_a0, rwkv_a2, rwkv_g2, rwkv_k_k, rwkv_k_a, rwkv_r_k, rwkv_ln_g, rwkv_ln_b,
              ret_decay_logit, ret_gn_g, ret_gn_b, cd_w_out, final_norm_g):
    h, hc = x, ctx
    s_lat, s_ctx = jax.nn.silu(c), jax.nn.silu(c_ctx)
    for layer in range(DEPTH):
        need_ctx = layer < DEPTH - 1
        mod_l = (s_lat @ mod_w[layer] + mod_b[layer])[:, None, :]
        mod_c = (s_ctx @ mod_w[layer] + mod_b[layer])[None, None, :]
        sh1, sc1, g1, sh2, sc2, g2 = jnp.split(mod_l, 6, axis=-1)
        csh1, csc1, cg1, csh2, csc2, cg2 = jnp.split(mod_c, 6, axis=-1)
        u = modulate(rms_norm(h, norm_mix_g[layer]), sh1, sc1)
        uc = modulate(rms_norm(hc, norm_mix_g[layer]), csh1, csc1)
        i = layer // 2
        if layer % 2 == 0:
            y, yc = mixer_ab(u, uc, ab_w_in[i], ab_q_norm[i], ab_k_norm[i], ab_conv_w[i], ab_conv_b[i],
                             ab_conv_norm_g[i], ab_conv_norm_b[i], ab_w_out[i], need_ctx)
        else:
            y, yc = mixer_cd(u, uc, cd_w_in[i], cd_shift_mu[i], rwkv_w0[i], rwkv_w2[i], rwkv_a0[i],
                             rwkv_a2[i], rwkv_g2[i], rwkv_k_k[i], rwkv_k_a[i], rwkv_r_k[i],
                             rwkv_ln_g[i], rwkv_ln_b[i], ret_decay_logit[i], ret_gn_g[i], ret_gn_b[i],
                             cd_w_out[i], need_ctx)
        h = h + g1 * y
        f = modulate(rms_norm(h, norm_ffn_g[layer]), sh2, sc2)
        h = h + g2 * conv_ffn(f, ffn_w_up[layer], ffn_conv_w[layer], ffn_conv_b[layer], ffn_w_down[layer])
        if need_ctx:
            hc = hc + cg1 * yc
            fc = modulate(rms_norm(hc, norm_ffn_g[layer]), csh2, csc2)
            hc = hc + cg2 * conv_ffn(fc, ffn_w_up[layer], ffn_conv_w[layer], ffn_conv_b[layer],
                                     ffn_w_down[layer])
    return rms_norm(h, final_norm_g)
```

```python
import functools

import jax
import jax.numpy as jnp
import numpy as np
from jax import lax
from jax.experimental import pallas as pl
from jax.experimental.pallas import tpu as pltpu

F32 = jnp.float32
BF16 = jnp.bfloat16

NORM_EPS = 1e-6
GRID_W = 64
ROPE_BASE = 10000.0
HEAD_DIM = 128
ATT_HEADS = 8
ATT_GROUP = 4
CONV_WIDTH = 1024
CONV_KERNEL = 31
RWKV_WIDTH = 1024
RWKV_HEAD = 64
RWKV_GN_EPS = 64e-5
RWKV_CHUNK = 64
LORA_IN = 128
GATE_LORA = 160
GATE_PAD = 384
RWKV_IN_PAD = 3584
RET_HEADS = 8
RET_V_DIM = 256
RET_CHUNK = 128
FFN_DIM = 5632
HALO = 16
LANES = 128

VMEM_LIMIT_BYTES = 52 * 1024 * 1024


def _cparams(*sem):
    return pltpu.CompilerParams(dimension_semantics=sem, vmem_limit_bytes=VMEM_LIMIT_BYTES)


def _dot(a, b):
    return jnp.dot(a, b, preferred_element_type=F32)


def _dot_nt(a, b):
    return lax.dot_general(a, b, (((1,), (1,)), ((), ())), preferred_element_type=F32)


def _split_dot(x, ones_bf16, terms):
    acc = None
    rem = x
    for _ in range(terms):
        piece = rem.astype(BF16)
        part = _dot(piece, ones_bf16)
        acc = part if acc is None else acc + part
        rem = rem - piece.astype(F32)
    return acc


def _ones_dot(ones_bf16, x, terms):
    acc = None
    rem = x
    for _ in range(terms):
        piece = rem.astype(BF16)
        part = _dot(ones_bf16, piece)
        acc = part if acc is None else acc + part
        rem = rem - piece.astype(F32)
    return acc


def _rms(x, g):
    return x * lax.rsqrt(jnp.mean(x * x, axis=-1, keepdims=True) + NORM_EPS) * g


def _norm_mod(x, g, shift, scale):
    return _rms(x, g) * (1.0 + scale) + shift


def _silu(x):
    return x * jax.nn.sigmoid(x)


def _rope(x, cos, sin_signed):
    n = x.shape[-1]
    lane = lax.broadcasted_iota(jnp.int32, x.shape, x.ndim - 1)
    partner = jnp.where((lane % 64) < 32, pltpu.roll(x, n - 32, x.ndim - 1), pltpu.roll(x, 32, x.ndim - 1))
    return x * cos + partner * sin_signed


def _rope_tables(n_tok):
    rows = n_tok // GRID_W
    row = jnp.repeat(jnp.arange(rows, dtype=F32), GRID_W)
    col = jnp.tile(jnp.arange(GRID_W, dtype=F32), rows)
    axis_dim = HEAD_DIM // 2
    inv_freq = ROPE_BASE ** (-jnp.arange(0, axis_dim, 2, dtype=F32) / axis_dim)
    ang = jnp.concatenate([row[:, None] * inv_freq, col[:, None] * inv_freq], axis=-1)
    cos, sin = jnp.cos(ang), jnp.sin(ang)
    c_r, c_c, s_r, s_c = cos[:, :32], cos[:, 32:], sin[:, :32], sin[:, 32:]
    cos_t = jnp.concatenate([c_r, c_r, c_c, c_c], axis=-1)
    sin_t = jnp.concatenate([-s_r, s_r, -s_c, s_c], axis=-1)
    return cos_t, sin_t


def _mod_kernel(c_ref, w_ref, b_ref, o_ref):
    s = _silu(c_ref[...]).astype(BF16)
    o_ref[0] = _dot(s, w_ref[0].astype(BF16)) + b_ref[0]


def _modulation(c_all, mod_w, mod_b):
    n_layers, d, n = mod_w.shape
    rows = c_all.shape[0]
    tn = 1024
    return pl.pallas_call(
        _mod_kernel,
        grid=(n_layers, n // tn),
        in_specs=[pl.BlockSpec((rows, d), lambda l, j: (0, 0)),
                  pl.BlockSpec((1, d, tn), lambda l, j: (l, 0, j)),
                  pl.BlockSpec((1, 1, tn), lambda l, j: (l, 0, j))],
        out_specs=pl.BlockSpec((1, rows, tn), lambda l, j: (l, 0, j)),
        out_shape=jax.ShapeDtypeStruct((n_layers, rows, n), F32),
        compiler_params=_cparams("parallel", "parallel"),
        name="modulation",
    )(c_all, mod_w, mod_b.reshape(n_layers, 1, n))


def _inproj_kernel(h_ref, g_ref, sh_ref, sc_ref, w_ref, o_ref, u_s):
    @pl.when(pl.program_id(2) == 0)
    def _():
        u_s[...] = _norm_mod(h_ref[0], g_ref[...], sh_ref[0], sc_ref[0]).astype(BF16)

    o_ref[0] = _dot(u_s[...], w_ref[...]).astype(o_ref.dtype)


def _norm_mod_matmul(h, g, shift, scale, w, out_dtype, name):
    bm, t, d = h.shape
    n = w.shape[1]
    tm = min(t, 1024)
    tn = 512
    return pl.pallas_call(
        _inproj_kernel,
        grid=(bm, t // tm, n // tn),
        in_specs=[pl.BlockSpec((1, tm, d), lambda b, i, j: (b, i, 0)),
                  pl.BlockSpec((1, d), lambda b, i, j: (0, 0)),
                  pl.BlockSpec((1, 1, d), lambda b, i, j: (b, 0, 0)),
                  pl.BlockSpec((1, 1, d), lambda b, i, j: (b, 0, 0)),
                  pl.BlockSpec((d, tn), lambda b, i, j: (0, j))],
        out_specs=pl.BlockSpec((1, tm, tn), lambda b, i, j: (b, i, j)),
        out_shape=jax.ShapeDtypeStruct((bm, t, n), out_dtype),
        scratch_shapes=[pltpu.VMEM((tm, d), BF16)],
        compiler_params=_cparams("parallel", "parallel", "arbitrary"),
        name=name,
    )(h, g.reshape(1, d), shift, scale, w)


def _outproj_kernel(*refs, n_lhs):
    lhs, ws = refs[:n_lhs], refs[n_lhs:2 * n_lhs]
    res_ref, gt_ref, o_ref = refs[2 * n_lhs:]
    acc = _dot(lhs[0][0], ws[0][...])
    for a, w in zip(lhs[1:], ws[1:]):
        acc = acc + _dot(a[0], w[...])
    o_ref[0] = res_ref[0] + gt_ref[0] * acc


def _matmul_gated_residual(lhs_list, w_list, res, gate, name):
    bm, t, d = res.shape
    tm = min(t, 1024)
    tn = 512
    in_specs = [pl.BlockSpec((1, tm, a.shape[-1]), lambda b, i, j: (b, i, 0)) for a in lhs_list]
    in_specs += [pl.BlockSpec((w.shape[0], tn), lambda b, i, j: (0, j)) for w in w_list]
    in_specs += [pl.BlockSpec((1, tm, tn), lambda b, i, j: (b, i, j)),
                 pl.BlockSpec((1, 1, tn), lambda b, i, j: (b, 0, j))]
    return pl.pallas_call(
        functools.partial(_outproj_kernel, n_lhs=len(lhs_list)),
        grid=(bm, t // tm, d // tn),
        in_specs=in_specs,
        out_specs=pl.BlockSpec((1, tm, tn), lambda b, i, j: (b, i, j)),
        out_shape=jax.ShapeDtypeStruct((bm, t, d), F32),
        compiler_params=_cparams("parallel", "parallel", "arbitrary"),
        name=name,
    )(*lhs_list, *w_list, res, gate)


def _ffn_kernel(h_ref, hp_ref, hn_ref, g_ref, sh_ref, sc_ref, gt_ref, wg_ref, wv_ref, cw_ref, cb_ref,
                wd_ref, fg_ref, o_ref, u_s, *, tm, seq_len, final_norm):
    i, j, nj = pl.program_id(1), pl.program_id(2), pl.num_programs(2)

    @pl.when(j == 0)
    def _():
        g, sh, sc = g_ref[...], sh_ref[0], sc_ref[0]
        u_s[0:HALO] = _norm_mod(hp_ref[0], g, sh, sc).astype(BF16)
        u_s[HALO:HALO + tm] = _norm_mod(h_ref[0], g, sh, sc).astype(BF16)
        u_s[HALO + tm:] = _norm_mod(hn_ref[0], g, sh, sc).astype(BF16)

    rows = tm + 2 * HALO
    gate_lin = _dot(u_s[...], wg_ref[...])
    val = _dot(u_s[HALO:HALO + tm], wv_ref[...])
    pos = (i * tm + lax.broadcasted_iota(jnp.int32, (tm, 1), 0)) % seq_len
    g_prev = jnp.where(pos != 0, pltpu.roll(gate_lin, 1, 0)[HALO:HALO + tm], 0.0)
    g_next = jnp.where(pos != seq_len - 1, pltpu.roll(gate_lin, rows - 1, 0)[HALO:HALO + tm], 0.0)
    g_cur = gate_lin[HALO:HALO + tm]
    cw = cw_ref[...]
    conv = cw[0:1] * g_prev + cw[1:2] * g_cur + cw[2:3] * g_next + cb_ref[...]
    part = _dot((_silu(conv) * val).astype(BF16), wd_ref[...])

    @pl.when(j == 0)
    def _():
        o_ref[0] = part

    @pl.when(j > 0)
    def _():
        o_ref[0] += part

    @pl.when(j == nj - 1)
    def _():
        y = h_ref[0] + gt_ref[0] * o_ref[0]
        if final_norm:
            y = _rms(y, fg_ref[...])
        o_ref[0] = y


def _conv_ffn(h, g, shift, scale, gate, w_up, conv_w, conv_b, w_down, final_g, seq_len, final_norm, name):
    bm, t, d = h.shape
    f = w_down.shape[0]
    tm = min(t, 512)
    tf = 512
    nh = t // HALO
    kern = functools.partial(_ffn_kernel, tm=tm, seq_len=seq_len, final_norm=final_norm)
    vec = pl.BlockSpec((1, 1, d), lambda b, i, j: (b, 0, 0))
    return pl.pallas_call(
        kern,
        grid=(bm, t // tm, f // tf),
        in_specs=[pl.BlockSpec((1, tm, d), lambda b, i, j: (b, i, 0)),
                  pl.BlockSpec((1, HALO, d), lambda b, i, j: (b, jnp.maximum(i * (tm // HALO) - 1, 0), 0)),
                  pl.BlockSpec((1, HALO, d), lambda b, i, j: (b, jnp.minimum((i + 1) * (tm // HALO), nh - 1), 0)),
                  pl.BlockSpec((1, d), lambda b, i, j: (0, 0)),
                  vec, vec, vec,
                  pl.BlockSpec((d, tf), lambda b, i, j: (0, j)),
                  pl.BlockSpec((d, tf), lambda b, i, j: (0, f // tf + j)),
                  pl.BlockSpec((3, tf), lambda b, i, j: (0, j)),
                  pl.BlockSpec((1, tf), lambda b, i, j: (0, j)),
                  pl.BlockSpec((tf, d), lambda b, i, j: (j, 0)),
                  pl.BlockSpec((1, d), lambda b, i, j: (0, 0))],
        out_specs=pl.BlockSpec((1, tm, d), lambda b, i, j: (b, i, 0)),
        out_shape=jax.ShapeDtypeStruct((bm, t, d), F32),
        scratch_shapes=[pltpu.VMEM((tm + 2 * HALO, d), BF16)],
        compiler_params=_cparams("parallel", "parallel", "arbitrary"),
        name=name,
    )(h, h, h, g.reshape(1, d), shift, scale, gate, w_up, w_up, conv_w, conv_b.reshape(1, f), w_down,
      final_g.reshape(1, d))


def _attn_kernel(*refs, has_lat):
    if has_lat:
        (q_ref, kc_ref, vc_ref, kl_ref, vl_ref, qn_ref, kn_ref, cq_ref, sq_ref, ck_ref, sk_ref,
         o_ref, kc_s, kl_s) = refs
    else:
        q_ref, kc_ref, vc_ref, qn_ref, kn_ref, o_ref, kc_s = refs

    @pl.when(pl.program_id(2) == 0)
    def _():
        kc_s[...] = _rms(kc_ref[0].astype(F32), kn_ref[...]).astype(BF16)
        if has_lat:
            kl = _rms(kl_ref[0].astype(F32), kn_ref[...])
            kl_s[...] = _rope(kl, ck_ref[...], sk_ref[...]).astype(BF16)

    scale = HEAD_DIM ** -0.5
    for hh in range(ATT_GROUP):
        cols = slice(hh * HEAD_DIM, (hh + 1) * HEAD_DIM)
        q = _rms(q_ref[0, :, cols].astype(F32), qn_ref[...])
        if has_lat:
            q = _rope(q, cq_ref[...], sq_ref[...])
        q = (q * scale).astype(BF16)
        s_c = _dot_nt(q, kc_s[...])
        m = jnp.max(s_c, axis=-1, keepdims=True)
        if has_lat:
            s_l = _dot_nt(q, kl_s[...])
            m = jnp.maximum(m, jnp.max(s_l, axis=-1, keepdims=True))
        p_c = jnp.exp(s_c - m)
        denom = jnp.sum(p_c, axis=-1, keepdims=True)
        acc = _dot(p_c.astype(BF16), vc_ref[0])
        if has_lat:
            p_l = jnp.exp(s_l - m)
            denom = denom + jnp.sum(p_l, axis=-1, keepdims=True)
            acc = acc + _dot(p_l.astype(BF16), vl_ref[0])
        o_ref[0, :, cols] = (acc / denom).astype(o_ref.dtype)


def _attention(p_q, p_ctx, p_lat, q_norm, k_norm, cos_t, sin_t):
    b, t, _ = p_q.shape
    n_c = p_ctx.shape[1]
    has_lat = p_lat is not None
    tq = min(t, 256)
    gw = ATT_GROUP * HEAD_DIM
    q_blk, k_blk, v_blk = 2048 // gw, (2048 + 1024) // HEAD_DIM, (2048 + 1024 + 256) // HEAD_DIM
    vec = pl.BlockSpec((1, HEAD_DIM), lambda bi, g, qi: (0, 0))
    in_specs = [pl.BlockSpec((1, tq, gw), lambda bi, g, qi: (bi, qi, q_blk + g)),
                pl.BlockSpec((1, n_c, HEAD_DIM), lambda bi, g, qi: (bi, 0, k_blk + g)),
                pl.BlockSpec((1, n_c, HEAD_DIM), lambda bi, g, qi: (bi, 0, v_blk + g))]
    args = [p_q, p_ctx, p_ctx]
    scratch = [pltpu.VMEM((n_c, HEAD_DIM), BF16)]
    if has_lat:
        n_l = p_lat.shape[1]
        in_specs += [pl.BlockSpec((1, n_l, HEAD_DIM), lambda bi, g, qi: (bi, 0, k_blk + g)),
                     pl.BlockSpec((1, n_l, HEAD_DIM), lambda bi, g, qi: (bi, 0, v_blk + g))]
        args += [p_lat, p_lat]
        scratch.append(pltpu.VMEM((n_l, HEAD_DIM), BF16))
    in_specs += [vec, vec]
    args += [q_norm.reshape(1, HEAD_DIM), k_norm.reshape(1, HEAD_DIM)]
    if has_lat:
        in_specs += [pl.BlockSpec((tq, HEAD_DIM), lambda bi, g, qi: (qi, 0)),
                     pl.BlockSpec((tq, HEAD_DIM), lambda bi, g, qi: (qi, 0)),
                     pl.BlockSpec((n_l, HEAD_DIM), lambda bi, g, qi: (0, 0)),
                     pl.BlockSpec((n_l, HEAD_DIM), lambda bi, g, qi: (0, 0))]
        args += [cos_t, sin_t, cos_t, sin_t]
    return pl.pallas_call(
        functools.partial(_attn_kernel, has_lat=has_lat),
        grid=(b, ATT_HEADS // ATT_GROUP, t // tq),
        in_specs=in_specs,
        out_specs=pl.BlockSpec((1, tq, gw), lambda bi, g, qi: (bi, qi, g)),
        out_shape=jax.ShapeDtypeStruct((b, t, ATT_HEADS * HEAD_DIM), BF16),
        scratch_shapes=scratch,
        compiler_params=_cparams("parallel", "parallel", "arbitrary"),
        name="attention_lat" if has_lat else "attention_ctx",
    )(*args)


def _conformer_kernel(x_ref, xp_ref, xn_ref, w_ref, b_ref, g_ref, be_ref, o_ref, xs, *, tt):
    i, n = pl.program_id(1), pl.num_programs(1)

    def glu(ref):
        v = ref[0].astype(F32)
        return v[:, :CONV_WIDTH] * jax.nn.sigmoid(v[:, CONV_WIDTH:])

    xs[0:HALO] = jnp.where(i > 0, glu(xp_ref), 0.0)
    xs[HALO:HALO + tt] = glu(x_ref)
    xs[HALO + tt:] = jnp.where(i < n - 1, glu(xn_ref), 0.0)
    pad = CONV_KERNEL // 2
    acc = jnp.zeros((tt, CONV_WIDTH), F32)
    for k in range(CONV_KERNEL):
        off = HALO - pad + k
        acc = acc + w_ref[k:k + 1, :] * xs[off:off + tt, :]
    h = acc + b_ref[...]
    hc = h - jnp.mean(h, axis=-1, keepdims=True)
    var = jnp.mean(hc * hc, axis=-1, keepdims=True)
    y = hc * lax.rsqrt(var + NORM_EPS) * g_ref[...] + be_ref[...]
    o_ref[0] = _silu(y).astype(o_ref.dtype)


def _conformer(p, conv_w, conv_b, norm_g, norm_b):
    b, t, _ = p.shape
    tt = min(t, 256)
    nh = t // HALO
    cw = CONV_WIDTH
    vec = pl.BlockSpec((1, cw), lambda bi, i: (0, 0))
    return pl.pallas_call(
        functools.partial(_conformer_kernel, tt=tt),
        grid=(b, t // tt),
        in_specs=[pl.BlockSpec((1, tt, 2 * cw), lambda bi, i: (bi, i, 0)),
                  pl.BlockSpec((1, HALO, 2 * cw), lambda bi, i: (bi, jnp.maximum(i * (tt // HALO) - 1, 0), 0)),
                  pl.BlockSpec((1, HALO, 2 * cw), lambda bi, i: (bi, jnp.minimum((i + 1) * (tt // HALO), nh - 1), 0)),
                  pl.BlockSpec((CONV_KERNEL, cw), lambda bi, i: (0, 0)),
                  vec, vec, vec],
        out_specs=pl.BlockSpec((1, tt, cw), lambda bi, i: (bi, i, 0)),
        out_shape=jax.ShapeDtypeStruct((b, t, cw), BF16),
        scratch_shapes=[pltpu.VMEM((tt + 2 * HALO, cw), F32)],
        compiler_params=_cparams("parallel", "parallel"),
        name="conformer",
    )(p, p, p, conv_w, conv_b.reshape(1, cw), norm_g.reshape(1, cw), norm_b.reshape(1, cw))


def _head_ones(n, width):
    r = lax.broadcasted_iota(jnp.int32, (n, n), 0) // width
    c = lax.broadcasted_iota(jnp.int32, (n, n), 1) // width
    return jnp.where(r == c, 1.0, 0.0).astype(BF16)


def _softplus(x):
    return jnp.maximum(x, 0.0) + jnp.log(1.0 + jnp.exp(-jnp.abs(x)))


def _rwkv_feat_kernel(z_ref, zp_ref, zn_ref, mu_ref, w0_ref, w2_ref, a0_ref, a2_ref, g2_ref, kk_ref, ka_ref,
                      r_o, v_o, kk_o, lw0_o, lw1_o, kd0_o, kd1_o, b0_o, b1_o, gate_o, zs, *, tt):
    i, n = pl.program_id(1), pl.num_programs(1)
    zs[0:HALO] = jnp.where(i > 0, zp_ref[0], 0.0)
    zs[HALO:HALO + tt] = z_ref[0]
    zs[HALO + tt:] = jnp.where(i < n - 1, zn_ref[0], 0.0)
    z = z_ref[0]
    z_prev = zs[HALO - 1:HALO - 1 + tt, :]
    z_next = zs[HALO + 1:HALO + 1 + tt, :]
    x = z + mu_ref[0:1, :] * (z_prev - z) + mu_ref[1:2, :] * (z_next - z)
    w = RWKV_WIDTH
    r, k, v = x[:, :w], x[:, w:2 * w], x[:, 2 * w:3 * w]
    lora = x[:, 3 * w:3 * w + LORA_IN]
    gate_in = x[:, 3 * w + LORA_IN:]
    r_o[0] = r
    v_o[0] = v
    kk = k * kk_ref[...]
    ones = _head_ones(LANES, RWKV_HEAD)
    sq = kk * kk
    ssum = jnp.concatenate([_split_dot(sq[:, c * LANES:(c + 1) * LANES], ones, 2) for c in range(w // LANES)], axis=-1)
    kk = kk * lax.rsqrt(ssum + 1e-12)
    kk_o[0] = kk
    lora_t = jnp.tanh(lora).astype(BF16)
    lora_b = lora.astype(BF16)
    for d, (lw_o, kd_o, b_o) in enumerate(((lw0_o, kd0_o, b0_o), (lw1_o, kd1_o, b1_o))):
        wl = -_softplus(-(w0_ref[d:d + 1, :] + _dot(lora_t, w2_ref[d]))) - 0.5
        lw_o[0] = -jnp.exp(wl)
        a = jax.nn.sigmoid(a0_ref[d:d + 1, :] + _dot(lora_b, a2_ref[d]))
        kd_o[0] = k * (1.0 + (a - 1.0) * ka_ref[...])
        b_o[0] = kk * a
    gate_o[0] = _dot(jax.nn.sigmoid(gate_in).astype(BF16), g2_ref[...])


def _rwkv_features(z, mu, w0, w2p, a0, a2p, g2p, k_k, k_a):
    b, t, zw = z.shape
    tt = min(t, 256)
    nh = t // HALO
    w = RWKV_WIDTH
    vec = pl.BlockSpec((1, w), lambda bi, i: (0, 0))
    out_spec = pl.BlockSpec((1, tt, w), lambda bi, i: (bi, i, 0))
    out = jax.ShapeDtypeStruct((b, t, w), F32)
    return pl.pallas_call(
        functools.partial(_rwkv_feat_kernel, tt=tt),
        grid=(b, t // tt),
        in_specs=[pl.BlockSpec((1, tt, zw), lambda bi, i: (bi, i, 0)),
                  pl.BlockSpec((1, HALO, zw), lambda bi, i: (bi, jnp.maximum(i * (tt // HALO) - 1, 0), 0)),
                  pl.BlockSpec((1, HALO, zw), lambda bi, i: (bi, jnp.minimum((i + 1) * (tt // HALO), nh - 1), 0)),
                  pl.BlockSpec((2, zw), lambda bi, i: (0, 0)),
                  pl.BlockSpec((2, w), lambda bi, i: (0, 0)),
                  pl.BlockSpec((2, LORA_IN, w), lambda bi, i: (0, 0, 0)),
                  pl.BlockSpec((2, w), lambda bi, i: (0, 0)),
                  pl.BlockSpec((2, LORA_IN, w), lambda bi, i: (0, 0, 0)),
                  pl.BlockSpec((GATE_PAD, w), lambda bi, i: (0, 0)),
                  vec, vec],
        out_specs=[out_spec] * 10,
        out_shape=[out] * 10,
        scratch_shapes=[pltpu.VMEM((tt + 2 * HALO, zw), F32)],
        compiler_params=_cparams("parallel", "parallel"),
        name="rwkv_features",
    )(z, z, z, mu, w0, w2p, a0, a2p, g2p, k_k.reshape(1, w), k_a.reshape(1, w))


def _rwkv_chunk(S, r, v, kk, lw, kd, beta, consts, want_y):
    tri_incl, incl2, strict2, low_lane, block_mask = consts
    c = v.shape[0]
    cs = _ones_dot(tri_incl, lw, 3)
    ce = cs - lw
    tot = jnp.sum(lw, axis=0, keepdims=True)
    e_neg = jnp.exp(-cs)
    e_rem = jnp.exp(tot - cs)
    kt = kk * jnp.exp(ce)
    kh = kd * e_neg
    bh = beta * e_neg
    kp = kd * e_rem
    bp = beta * e_rem

    def stack_heads(x):
        return jnp.concatenate([jnp.where(low_lane, x, 0.0), jnp.where(low_lane, 0.0, x)], axis=0).astype(BF16)

    def unstack(x):
        return jnp.where(low_lane, x[:c], x[c:])

    s_b = S.astype(BF16)
    v_b = v.astype(BF16)
    v2 = jnp.concatenate([v_b, v_b], axis=0)
    kt_st = stack_heads(kt)
    bh_st = stack_heads(bh)
    kh_st = stack_heads(kh)
    a_kb = _dot_nt(kt_st, bh_st) * strict2
    a_kk = _dot_nt(kt_st, kh_st) * strict2
    ks = _dot_nt(kt.astype(BF16), s_b)
    rhs = jnp.concatenate([ks, ks], axis=0) + _dot(a_kk.astype(BF16), v2)
    p = -a_kb
    x = p
    for _ in range(int(np.log2(c)) - 1):
        p_b = p.astype(BF16)
        p = _dot(p_b, p_b)
        x = x + p + _dot(x.astype(BF16), p.astype(BF16))
    u_st = -(rhs + _dot(x.astype(BF16), rhs.astype(BF16)))
    u = unstack(u_st)
    u_b = u.astype(BF16)
    y = None
    if want_y:
        rt = r * jnp.exp(cs)
        rt_st = stack_heads(rt)
        a_rb = (_dot_nt(rt_st, bh_st) * incl2).astype(BF16)
        a_rk = (_dot_nt(rt_st, kh_st) * incl2).astype(BF16)
        y_st = _dot(jnp.concatenate([a_rb, a_rk], axis=1), jnp.concatenate([u_b, u_b, v2], axis=0))
        y = _dot_nt(rt.astype(BF16), s_b) + unstack(y_st)
    left = jnp.concatenate([u, v], axis=0).T.astype(BF16)
    right = jnp.concatenate([bp, kp], axis=0).astype(BF16)
    S_new = S * jnp.exp(tot) + _dot(left, right) * block_mask
    return S_new, y


def _rwkv_consts(c, reverse):
    t = lax.broadcasted_iota(jnp.int32, (c, c), 0)
    s = lax.broadcasted_iota(jnp.int32, (c, c), 1)
    incl = (s >= t) if reverse else (s <= t)
    tri_incl = jnp.where(incl, 1.0, 0.0).astype(BF16)
    t2 = lax.broadcasted_iota(jnp.int32, (2 * c, 2 * c), 0)
    s2 = lax.broadcasted_iota(jnp.int32, (2 * c, 2 * c), 1)
    same = (t2 // c) == (s2 // c)
    order = (s2 >= t2) if reverse else (s2 <= t2)
    incl2 = jnp.where(same & order, 1.0, 0.0)
    strict2 = jnp.where(same & order & (s2 != t2), 1.0, 0.0)
    low_lane = lax.broadcasted_iota(jnp.int32, (1, LANES), 1) < RWKV_HEAD
    bi = lax.broadcasted_iota(jnp.int32, (LANES, LANES), 0) // RWKV_HEAD
    bj = lax.broadcasted_iota(jnp.int32, (LANES, LANES), 1) // RWKV_HEAD
    block_mask = jnp.where(bi == bj, 1.0, 0.0)
    return tri_incl, incl2, strict2, low_lane, block_mask


def _rwkv_kernel(r_l, v_l, kk_l, lw0_l, lw1_l, kd0_l, kd1_l, b0_l, b1_l, gate_l,
                 v_c, kk_c, lw0_c, lw1_c, kd0_c, kd1_c, b0_c, b1_c,
                 rk_ref, lng_ref, lnb_ref, o_ref, acc_s):
    c = RWKV_CHUNK
    n_l, n_c = r_l.shape[1] // c, v_c.shape[1] // c
    acc_s[...] = jnp.zeros_like(acc_s)
    for d, (lw_lr, kd_lr, b_lr, lw_cr, kd_cr, b_cr) in enumerate(
            ((lw0_l, kd0_l, b0_l, lw0_c, kd0_c, b0_c), (lw1_l, kd1_l, b1_l, lw1_c, kd1_c, b1_c))):
        reverse = d == 1
        consts = _rwkv_consts(c, reverse)

        def ctx_body(ci, S):
            cc = (n_c - 1 - ci) if reverse else ci
            rows = pl.ds(pl.multiple_of(cc * c, c), c)
            S, _ = _rwkv_chunk(S, None, v_c[0, rows, :], kk_c[0, rows, :], lw_cr[0, rows, :], kd_cr[0, rows, :],
                               b_cr[0, rows, :], consts, False)
            return S

        def lat_body(ci, S):
            cc = (n_l - 1 - ci) if reverse else ci
            rows = pl.ds(pl.multiple_of(cc * c, c), c)
            S, y = _rwkv_chunk(S, r_l[0, rows, :], v_l[0, rows, :], kk_l[0, rows, :], lw_lr[0, rows, :],
                               kd_lr[0, rows, :], b_lr[0, rows, :], consts, True)
            acc_s[rows, :] += y
            return S

        S = lax.fori_loop(0, n_c, ctx_body, jnp.zeros((LANES, LANES), F32))
        lax.fori_loop(0, n_l, lat_body, S)

    ones = _head_ones(LANES, RWKV_HEAD)
    wkv = acc_s[...]
    inv_n = 1.0 / RWKV_HEAD
    mean = _split_dot(wkv, ones, 2) * inv_n
    xc = wkv - mean
    var = _split_dot(xc * xc, ones, 2) * inv_n
    y = xc * lax.rsqrt(var + RWKV_GN_EPS) * lng_ref[...] + lnb_ref[...]
    r = r_l[0]
    rk = rk_ref[...]
    bonus = (_split_dot(r * kd0_l[0] * rk, ones, 2) + _split_dot(r * kd1_l[0] * rk, ones, 2)) * v_l[0]
    o_ref[0] = ((y + bonus) * gate_l[0]).astype(o_ref.dtype)


def _rwkv_mix(feat_l, feat_c, r_k, ln_g, ln_b):
    r_l, v_l, kk_l, lw0_l, lw1_l, kd0_l, kd1_l, b0_l, b1_l, gate_l = feat_l
    _, v_c, kk_c, lw0_c, lw1_c, kd0_c, kd1_c, b0_c, b1_c, _ = feat_c
    b, t, w = r_l.shape
    t_c = v_c.shape[1]
    lat = pl.BlockSpec((1, t, LANES), lambda bi, p: (bi, 0, p))
    ctx = pl.BlockSpec((1, t_c, LANES), lambda bi, p: (bi, 0, p))
    vec = pl.BlockSpec((1, LANES), lambda bi, p: (0, p))
    return pl.pallas_call(
        _rwkv_kernel,
        grid=(b, w // LANES),
        in_specs=[lat] * 10 + [ctx] * 8 + [vec] * 3,
        out_specs=lat,
        out_shape=jax.ShapeDtypeStruct((b, t, w), BF16),
        scratch_shapes=[pltpu.VMEM((t, LANES), F32)],
        compiler_params=_cparams("parallel", "parallel"),
        name="rwkv_mix",
    )(r_l, v_l, kk_l, lw0_l, lw1_l, kd0_l, kd1_l, b0_l, b1_l, gate_l,
      v_c, kk_c, lw0_c, lw1_c, kd0_c, kd1_c, b0_c, b1_c,
      r_k.reshape(1, w), ln_g.reshape(1, w), ln_b.reshape(1, w))


def _retention_kernel(dl_ref, q_l, k_l, v_l, rg_l, k_c, v_c, cos_ref, sin_ref, gg_ref, gb_ref, o_ref,
                      q_s, k_s, acc_s):
    c = RET_CHUNK
    n_l, n_c = q_l.shape[1] // c, k_c.shape[1] // c
    h = pl.program_id(1)
    k_scale = HEAD_DIM ** -0.5
    q_s[...] = _rope(q_l[0].astype(F32), cos_ref[...], sin_ref[...]).astype(BF16)
    k_s[...] = (_rope(k_l[0].astype(F32), cos_ref[...], sin_ref[...]) * k_scale).astype(BF16)
    ti = lax.broadcasted_iota(jnp.int32, (c, c), 0).astype(F32)
    si = lax.broadcasted_iota(jnp.int32, (c, c), 1).astype(F32)
    idx = lax.broadcasted_iota(jnp.int32, (c, 1), 0).astype(F32)
    for d in range(2):
        reverse = d == 1
        lg_row = jax.nn.log_sigmoid(jnp.full((1, LANES), dl_ref[d, h], F32))
        lg = lg_row[:, :1]
        dist = (si - ti) if reverse else (ti - si)
        inner = jnp.where(dist >= 0, jnp.exp(lg * jnp.maximum(dist, 0.0)), 0.0)
        pos = (c - 1.0 - idx) if reverse else idx
        q_decay = jnp.exp(lg * (pos + 1.0))
        k_decay = jnp.exp(lg * (c - 1.0 - pos))
        chunk_decay = jnp.exp(lg * c)

        def state_update(R, k_f32, v_b):
            kd_t = (k_f32 * k_decay).T.astype(BF16)
            return R * chunk_decay + _dot(kd_t, v_b)

        def ctx_body(ci, R):
            cc = (n_c - 1 - ci) if reverse else ci
            rows = pl.ds(pl.multiple_of(cc * c, c), c)
            return state_update(R, k_c[0, rows, :].astype(F32) * k_scale, v_c[0, rows, :])

        def lat_body(ci, R):
            cc = (n_l - 1 - ci) if reverse else ci
            rows = pl.ds(pl.multiple_of(cc * c, c), c)
            q_b, k_b, v_b = q_s[rows, :], k_s[rows, :], v_l[0, rows, :]
            s = _dot_nt(q_b, k_b) * inner
            o = _dot(s.astype(BF16), v_b) + _dot(q_b, R.astype(BF16)) * q_decay
            if reverse:
                acc_s[rows, :] += o
            else:
                acc_s[rows, :] = o
            return state_update(R, k_b.astype(F32), v_b)

        R = lax.fori_loop(0, n_c, ctx_body, jnp.zeros((HEAD_DIM, RET_V_DIM), F32))
        lax.fori_loop(0, n_l, lat_body, R)

    x = acc_s[...]
    xc = x - jnp.mean(x, axis=-1, keepdims=True)
    var = jnp.mean(xc * xc, axis=-1, keepdims=True)
    y = xc * lax.rsqrt(var + NORM_EPS) * gg_ref[...] + gb_ref[...]
    o_ref[0] = (y * _silu(rg_l[0].astype(F32))).astype(o_ref.dtype)


def _retention(p_l, p_c, decay_logit, gn_g, gn_b, cos_t, sin_t):
    b, t, _ = p_l.shape
    t_c = p_c.shape[1]
    nh, dk, dv = RET_HEADS, HEAD_DIM, RET_V_DIM
    k_blk, v_blk, g_blk = nh, (2 * nh * dk) // dv, (2 * nh * dk + nh * dv) // dv
    return pl.pallas_call(
        _retention_kernel,
        grid=(b, nh),
        in_specs=[pl.BlockSpec(memory_space=pltpu.SMEM),
                  pl.BlockSpec((1, t, dk), lambda bi, h: (bi, 0, h)),
                  pl.BlockSpec((1, t, dk), lambda bi, h: (bi, 0, k_blk + h)),
                  pl.BlockSpec((1, t, dv), lambda bi, h: (bi, 0, v_blk + h)),
                  pl.BlockSpec((1, t, dv), lambda bi, h: (bi, 0, g_blk + h)),
                  pl.BlockSpec((1, t_c, dk), lambda bi, h: (bi, 0, k_blk + h)),
                  pl.BlockSpec((1, t_c, dv), lambda bi, h: (bi, 0, v_blk + h)),
                  pl.BlockSpec((t, dk), lambda bi, h: (0, 0)),
                  pl.BlockSpec((t, dk), lambda bi, h: (0, 0)),
                  pl.BlockSpec((1, dv), lambda bi, h: (0, h)),
                  pl.BlockSpec((1, dv), lambda bi, h: (0, h))],
        out_specs=pl.BlockSpec((1, t, dv), lambda bi, h: (bi, 0, h)),
        out_shape=jax.ShapeDtypeStruct((b, t, nh * dv), BF16),
        scratch_shapes=[pltpu.VMEM((t, dk), BF16), pltpu.VMEM((t, dk), BF16), pltpu.VMEM((t, dv), F32)],
        compiler_params=_cparams("parallel", "parallel"),
        name="retention",
    )(decay_logit, p_l, p_l, p_l, p_l, p_c, p_c, cos_t, sin_t, gn_g.reshape(1, nh * dv), gn_b.reshape(1, nh * dv))


def kernel(x, c, ctx, c_ctx, mod_w, mod_b, norm_mix_g, norm_ffn_g, ffn_w_up, ffn_conv_w, ffn_conv_b, ffn_w_down, ab_w_in, ab_q_norm, ab_k_norm, ab_conv_w, ab_conv_b, ab_conv_norm_g, ab_conv_norm_b, ab_w_out, cd_w_in, cd_shift_mu, rwkv_w0, rwkv_w2, rwkv_a0, rwkv_a2, rwkv_g2, rwkv_k_k, rwkv_k_a, rwkv_r_k, rwkv_ln_g, rwkv_ln_b, ret_decay_logit, ret_gn_g, ret_gn_b, cd_w_out, final_norm_g):
    b, t, d = x.shape
    t_c = ctx.shape[1]
    att_w = ATT_HEADS * HEAD_DIM
    kv_w = att_w // ATT_GROUP

    pad_rows = (-(b + 1)) % 8
    c_all = jnp.concatenate([c, c_ctx[None, :], jnp.zeros((pad_rows, d), F32)], axis=0)
    mod = _modulation(c_all, mod_w, mod_b)

    def mod_vectors(layer):
        lat = [m.reshape(b, 1, d) for m in jnp.split(mod[layer, :b], 6, axis=-1)]
        cx = [m.reshape(1, 1, d) for m in jnp.split(mod[layer, b:b + 1], 6, axis=-1)]
        return lat, cx

    cos_t, sin_t = _rope_tables(t)
    h = x
    hc = ctx.reshape(1, b * t_c, d)

    (sh1, sc1, g1, sh2, sc2, g2), (csh1, csc1, cg1, csh2, csc2, cg2) = mod_vectors(0)
    w_in = ab_w_in[0]
    glu0 = att_w + 2 * kv_w
    w_in = jnp.concatenate([w_in[:, glu0:], w_in[:, :glu0]], axis=1).astype(BF16)
    p_l = _norm_mod_matmul(h, norm_mix_g[0], sh1, sc1, w_in, BF16, "ab_in_lat")
    p_c = _norm_mod_matmul(hc, norm_mix_g[0], csh1, csc1, w_in, BF16, "ab_in_ctx").reshape(b, t_c, -1)
    att_l = _attention(p_l, p_c, p_l, ab_q_norm[0], ab_k_norm[0], cos_t, sin_t)
    att_c = _attention(p_c, p_c, None, ab_q_norm[0], ab_k_norm[0], None, None)
    cv_l = _conformer(p_l, ab_conv_w[0], ab_conv_b[0], ab_conv_norm_g[0], ab_conv_norm_b[0])
    cv_c = _conformer(p_c, ab_conv_w[0], ab_conv_b[0], ab_conv_norm_g[0], ab_conv_norm_b[0])
    w_out = ab_w_out[0].astype(BF16)
    w_o = [w_out[:att_w], w_out[att_w:]]
    h = _matmul_gated_residual([att_l, cv_l], w_o, h, g1, "ab_out_lat")
    hc = _matmul_gated_residual([att_c.reshape(1, b * t_c, -1), cv_c.reshape(1, b * t_c, -1)], w_o, hc, cg1,
                                "ab_out_ctx")
    w_up, w_down = ffn_w_up[0].astype(BF16), ffn_w_down[0].astype(BF16)
    h = _conv_ffn(h, norm_ffn_g[0], sh2, sc2, g2, w_up, ffn_conv_w[0], ffn_conv_b[0], w_down, final_norm_g,
                  t, False, "ffn0_lat")
    hc = _conv_ffn(hc, norm_ffn_g[0], csh2, csc2, cg2, w_up, ffn_conv_w[0], ffn_conv_b[0], w_down,
                   final_norm_g, t_c, False, "ffn0_ctx")

    (sh1, sc1, g1, sh2, sc2, g2), (csh1, csc1, _, _, _, _) = mod_vectors(1)
    rw_in = 3 * RWKV_WIDTH + 2 * RWKV_HEAD + GATE_LORA
    zpad = RWKV_IN_PAD - rw_in
    w_z = jnp.pad(cd_w_in[0][:, :rw_in], ((0, 0), (0, zpad))).astype(BF16)
    w_ret = cd_w_in[0][:, rw_in:].astype(BF16)
    z_l = _norm_mod_matmul(h, norm_mix_g[1], sh1, sc1, w_z, F32, "cd_in_z_lat")
    z_c = _norm_mod_matmul(hc, norm_mix_g[1], csh1, csc1, w_z, F32, "cd_in_z_ctx").reshape(b, t_c, -1)
    rp_l = _norm_mod_matmul(h, norm_mix_g[1], sh1, sc1, w_ret, BF16, "cd_in_ret_lat")
    rp_c = _norm_mod_matmul(hc, norm_mix_g[1], csh1, csc1, w_ret, BF16, "cd_in_ret_ctx").reshape(b, t_c, -1)

    mu = jnp.pad(cd_shift_mu[0], ((0, 0), (0, zpad)))
    lora_pad = LORA_IN - RWKV_HEAD
    w2p = jnp.pad(rwkv_w2[0], ((0, 0), (0, lora_pad), (0, 0))).astype(BF16)
    a2p = jnp.pad(rwkv_a2[0], ((0, 0), (lora_pad, 0), (0, 0))).astype(BF16)
    g2p = jnp.pad(rwkv_g2[0], ((0, GATE_PAD - GATE_LORA), (0, 0))).astype(BF16)
    feat_l = _rwkv_features(z_l, mu, rwkv_w0[0], w2p, rwkv_a0[0], a2p, g2p, rwkv_k_k[0], rwkv_k_a[0])
    feat_c = _rwkv_features(z_c, mu, rwkv_w0[0], w2p, rwkv_a0[0], a2p, g2p, rwkv_k_k[0], rwkv_k_a[0])
    y_c = _rwkv_mix(feat_l, feat_c, rwkv_r_k[0], rwkv_ln_g[0], rwkv_ln_b[0])
    y_d = _retention(rp_l, rp_c, ret_decay_logit[0], ret_gn_g[0], ret_gn_b[0], cos_t, sin_t)
    w_out = cd_w_out[0].astype(BF16)
    h = _matmul_gated_residual([y_c, y_d], [w_out[:RWKV_WIDTH], w_out[RWKV_WIDTH:]], h, g1, "cd_out_lat")
    w_up, w_down = ffn_w_up[1].astype(BF16), ffn_w_down[1].astype(BF16)
    return _conv_ffn(h, norm_ffn_g[1], sh2, sc2, g2, w_up, ffn_conv_w[1], ffn_conv_b[1], w_down, final_norm_g,
                     t, True, "ffn1_lat")
```

```python
import functools

import jax
import jax.numpy as jnp
import numpy as np
from jax import lax
from jax.experimental import pallas as pl
from jax.experimental.pallas import tpu as pltpu

F32 = jnp.float32
BF16 = jnp.bfloat16

NORM_EPS = 1e-6
GRID_W = 64
ROPE_BASE = 10000.0
HEAD_DIM = 128
ATT_HEADS = 8
ATT_GROUP = 4
CONV_WIDTH = 1024
CONV_KERNEL = 31
RWKV_WIDTH = 1024
RWKV_HEAD = 64
RWKV_GN_EPS = 64e-5
RWKV_CHUNK = 64
RWKV_GROUP = 4
LORA_IN = 128
GATE_LORA = 160
GATE_PAD = 384
RWKV_IN_PAD = 3584
RET_HEADS = 8
RET_V_DIM = 256
RET_CHUNK = 128
FFN_DIM = 5632
HALO = 16
LANES = 128

VMEM_LIMIT_BYTES = 52 * 1024 * 1024


def _cparams(*sem):
    return pltpu.CompilerParams(dimension_semantics=sem, vmem_limit_bytes=VMEM_LIMIT_BYTES)


def _dot(a, b):
    return jnp.dot(a, b, preferred_element_type=F32)


def _dot_nt(a, b):
    return lax.dot_general(a, b, (((1,), (1,)), ((), ())), preferred_element_type=F32)


def _split_dot(x, ones_bf16, terms):
    acc = None
    rem = x
    for _ in range(terms):
        piece = rem.astype(BF16)
        part = _dot(piece, ones_bf16)
        acc = part if acc is None else acc + part
        rem = rem - piece.astype(F32)
    return acc


def _ones_dot(ones_bf16, x, terms):
    acc = None
    rem = x
    for _ in range(terms):
        piece = rem.astype(BF16)
        part = _dot(ones_bf16, piece)
        acc = part if acc is None else acc + part
        rem = rem - piece.astype(F32)
    return acc


def _rms(x, g):
    return x * lax.rsqrt(jnp.mean(x * x, axis=-1, keepdims=True) + NORM_EPS) * g


def _norm_mod(x, g, shift, scale):
    return _rms(x, g) * (1.0 + scale) + shift


def _silu(x):
    return x * jax.nn.sigmoid(x)


def _rope(x, cos, sin_signed):
    n = x.shape[-1]
    lane = lax.broadcasted_iota(jnp.int32, x.shape, x.ndim - 1)
    partner = jnp.where((lane % 64) < 32, pltpu.roll(x, n - 32, x.ndim - 1), pltpu.roll(x, 32, x.ndim - 1))
    return x * cos + partner * sin_signed


def _rope_tables(n_tok):
    rows = n_tok // GRID_W
    row = jnp.repeat(jnp.arange(rows, dtype=F32), GRID_W)
    col = jnp.tile(jnp.arange(GRID_W, dtype=F32), rows)
    axis_dim = HEAD_DIM // 2
    inv_freq = ROPE_BASE ** (-jnp.arange(0, axis_dim, 2, dtype=F32) / axis_dim)
    ang = jnp.concatenate([row[:, None] * inv_freq, col[:, None] * inv_freq], axis=-1)
    cos, sin = jnp.cos(ang), jnp.sin(ang)
    c_r, c_c, s_r, s_c = cos[:, :32], cos[:, 32:], sin[:, :32], sin[:, 32:]
    cos_t = jnp.concatenate([c_r, c_r, c_c, c_c], axis=-1)
    sin_t = jnp.concatenate([-s_r, s_r, -s_c, s_c], axis=-1)
    return cos_t, sin_t


def _mod_kernel(c_ref, w_ref, b_ref, o_ref):
    s = _silu(c_ref[...]).astype(BF16)
    o_ref[0] = _dot(s, w_ref[0].astype(BF16)) + b_ref[0]


def _modulation(c_all, mod_w, mod_b):
    n_layers, d, n = mod_w.shape
    rows = c_all.shape[0]
    tn = 1024
    return pl.pallas_call(
        _mod_kernel,
        grid=(n_layers, n // tn),
        in_specs=[pl.BlockSpec((rows, d), lambda l, j: (0, 0)),
                  pl.BlockSpec((1, d, tn), lambda l, j: (l, 0, j)),
                  pl.BlockSpec((1, 1, tn), lambda l, j: (l, 0, j))],
        out_specs=pl.BlockSpec((1, rows, tn), lambda l, j: (l, 0, j)),
        out_shape=jax.ShapeDtypeStruct((n_layers, rows, n), F32),
        compiler_params=_cparams("parallel", "parallel"),
        name="modulation",
    )(c_all, mod_w, mod_b.reshape(n_layers, 1, n))


def _inproj_kernel(h_ref, g_ref, sh_ref, sc_ref, w_ref, o_ref, u_s):
    @pl.when(pl.program_id(2) == 0)
    def _():
        u_s[...] = _norm_mod(h_ref[0], g_ref[...], sh_ref[0], sc_ref[0]).astype(BF16)

    o_ref[0] = _dot(u_s[...], w_ref[...]).astype(o_ref.dtype)


def _norm_mod_matmul(h, g, shift, scale, w, out_dtype, name):
    bm, t, d = h.shape
    n = w.shape[1]
    tm = min(t, 1024)
    tn = 512
    return pl.pallas_call(
        _inproj_kernel,
        grid=(bm, t // tm, n // tn),
        in_specs=[pl.BlockSpec((1, tm, d), lambda b, i, j: (b, i, 0)),
                  pl.BlockSpec((1, d), lambda b, i, j: (0, 0)),
                  pl.BlockSpec((1, 1, d), lambda b, i, j: (b, 0, 0)),
                  pl.BlockSpec((1, 1, d), lambda b, i, j: (b, 0, 0)),
                  pl.BlockSpec((d, tn), lambda b, i, j: (0, j))],
        out_specs=pl.BlockSpec((1, tm, tn), lambda b, i, j: (b, i, j)),
        out_shape=jax.ShapeDtypeStruct((bm, t, n), out_dtype),
        scratch_shapes=[pltpu.VMEM((tm, d), BF16)],
        compiler_params=_cparams("parallel", "parallel", "arbitrary"),
        name=name,
    )(h, g.reshape(1, d), shift, scale, w)


def _outproj_kernel(*refs, n_lhs):
    lhs, ws = refs[:n_lhs], refs[n_lhs:2 * n_lhs]
    res_ref, gt_ref, o_ref = refs[2 * n_lhs:]
    acc = _dot(lhs[0][0], ws[0][...])
    for a, w in zip(lhs[1:], ws[1:]):
        acc = acc + _dot(a[0], w[...])
    o_ref[0] = res_ref[0] + gt_ref[0] * acc


def _matmul_gated_residual(lhs_list, w_list, res, gate, name):
    bm, t, d = res.shape
    tm = min(t, 1024)
    tn = 512
    in_specs = [pl.BlockSpec((1, tm, a.shape[-1]), lambda b, i, j: (b, i, 0)) for a in lhs_list]
    in_specs += [pl.BlockSpec((w.shape[0], tn), lambda b, i, j: (0, j)) for w in w_list]
    in_specs += [pl.BlockSpec((1, tm, tn), lambda b, i, j: (b, i, j)),
                 pl.BlockSpec((1, 1, tn), lambda b, i, j: (b, 0, j))]
    return pl.pallas_call(
        functools.partial(_outproj_kernel, n_lhs=len(lhs_list)),
        grid=(bm, t // tm, d // tn),
        in_specs=in_specs,
        out_specs=pl.BlockSpec((1, tm, tn), lambda b, i, j: (b, i, j)),
        out_shape=jax.ShapeDtypeStruct((bm, t, d), F32),
        compiler_params=_cparams("parallel", "parallel", "arbitrary"),
        name=name,
    )(*lhs_list, *w_list, res, gate)


def _ffn_kernel(h_ref, hp_ref, hn_ref, g_ref, sh_ref, sc_ref, gt_ref, wg_ref, wv_ref, cw_ref, cb_ref,
                wd_ref, fg_ref, o_ref, u_s, *, tm, seq_len, final_norm):
    i, j, nj = pl.program_id(1), pl.program_id(2), pl.num_programs(2)

    @pl.when(j == 0)
    def _():
        g, sh, sc = g_ref[...], sh_ref[0], sc_ref[0]
        u_s[0:HALO] = _norm_mod(hp_ref[0], g, sh, sc).astype(BF16)
        u_s[HALO:HALO + tm] = _norm_mod(h_ref[0], g, sh, sc).astype(BF16)
        u_s[HALO + tm:] = _norm_mod(hn_ref[0], g, sh, sc).astype(BF16)

    rows = tm + 2 * HALO
    gate_lin = _dot(u_s[...], wg_ref[...])
    val = _dot(u_s[HALO:HALO + tm], wv_ref[...])
    pos = (i * tm + lax.broadcasted_iota(jnp.int32, (tm, 1), 0)) % seq_len
    g_prev = jnp.where(pos != 0, pltpu.roll(gate_lin, 1, 0)[HALO:HALO + tm], 0.0)
    g_next = jnp.where(pos != seq_len - 1, pltpu.roll(gate_lin, rows - 1, 0)[HALO:HALO + tm], 0.0)
    g_cur = gate_lin[HALO:HALO + tm]
    cw = cw_ref[...]
    conv = cw[0:1] * g_prev + cw[1:2] * g_cur + cw[2:3] * g_next + cb_ref[...]
    part = _dot((_silu(conv) * val).astype(BF16), wd_ref[...])

    @pl.when(j == 0)
    def _():
        o_ref[0] = part

    @pl.when(j > 0)
    def _():
        o_ref[0] += part

    @pl.when(j == nj - 1)
    def _():
        y = h_ref[0] + gt_ref[0] * o_ref[0]
        if final_norm:
            y = _rms(y, fg_ref[...])
        o_ref[0] = y


def _conv_ffn(h, g, shift, scale, gate, w_up, conv_w, conv_b, w_down, final_g, seq_len, final_norm, name):
    bm, t, d = h.shape
    f = w_down.shape[0]
    tm = min(t, 512)
    tf = 512
    nh = t // HALO
    kern = functools.partial(_ffn_kernel, tm=tm, seq_len=seq_len, final_norm=final_norm)
    vec = pl.BlockSpec((1, 1, d), lambda b, i, j: (b, 0, 0))
    return pl.pallas_call(
        kern,
        grid=(bm, t // tm, f // tf),
        in_specs=[pl.BlockSpec((1, tm, d), lambda b, i, j: (b, i, 0)),
                  pl.BlockSpec((1, HALO, d), lambda b, i, j: (b, jnp.maximum(i * (tm // HALO) - 1, 0), 0)),
                  pl.BlockSpec((1, HALO, d), lambda b, i, j: (b, jnp.minimum((i + 1) * (tm // HALO), nh - 1), 0)),
                  pl.BlockSpec((1, d), lambda b, i, j: (0, 0)),
                  vec, vec, vec,
                  pl.BlockSpec((d, tf), lambda b, i, j: (0, j)),
                  pl.BlockSpec((d, tf), lambda b, i, j: (0, f // tf + j)),
                  pl.BlockSpec((3, tf), lambda b, i, j: (0, j)),
                  pl.BlockSpec((1, tf), lambda b, i, j: (0, j)),
                  pl.BlockSpec((tf, d), lambda b, i, j: (j, 0)),
                  pl.BlockSpec((1, d), lambda b, i, j: (0, 0))],
        out_specs=pl.BlockSpec((1, tm, d), lambda b, i, j: (b, i, 0)),
        out_shape=jax.ShapeDtypeStruct((bm, t, d), F32),
        scratch_shapes=[pltpu.VMEM((tm + 2 * HALO, d), BF16)],
        compiler_params=_cparams("parallel", "parallel", "arbitrary"),
        name=name,
    )(h, h, h, g.reshape(1, d), shift, scale, gate, w_up, w_up, conv_w, conv_b.reshape(1, f), w_down,
      final_g.reshape(1, d))


def _attn_kernel(*refs, has_lat):
    if has_lat:
        (q_ref, kc_ref, vc_ref, kl_ref, vl_ref, qn_ref, kn_ref, cq_ref, sq_ref, ck_ref, sk_ref,
         o_ref, kc_s, kl_s) = refs
    else:
        q_ref, kc_ref, vc_ref, qn_ref, kn_ref, o_ref, kc_s = refs

    @pl.when(pl.program_id(2) == 0)
    def _():
        kc_s[...] = _rms(kc_ref[0].astype(F32), kn_ref[...]).astype(BF16)
        if has_lat:
            kl = _rms(kl_ref[0].astype(F32), kn_ref[...])
            kl_s[...] = _rope(kl, ck_ref[...], sk_ref[...]).astype(BF16)

    scale = HEAD_DIM ** -0.5
    for hh in range(ATT_GROUP):
        cols = slice(hh * HEAD_DIM, (hh + 1) * HEAD_DIM)
        q = _rms(q_ref[0, :, cols].astype(F32), qn_ref[...])
        if has_lat:
            q = _rope(q, cq_ref[...], sq_ref[...])
        q = (q * scale).astype(BF16)
        s_c = _dot_nt(q, kc_s[...])
        m = jnp.max(s_c, axis=-1, keepdims=True)
        if has_lat:
            s_l = _dot_nt(q, kl_s[...])
            m = jnp.maximum(m, jnp.max(s_l, axis=-1, keepdims=True))
        p_c = jnp.exp(s_c - m)
        denom = jnp.sum(p_c, axis=-1, keepdims=True)
        acc = _dot(p_c.astype(BF16), vc_ref[0])
        if has_lat:
            p_l = jnp.exp(s_l - m)
            denom = denom + jnp.sum(p_l, axis=-1, keepdims=True)
            acc = acc + _dot(p_l.astype(BF16), vl_ref[0])
        o_ref[0, :, cols] = (acc / denom).astype(o_ref.dtype)


def _attention(p_q, p_ctx, p_lat, q_norm, k_norm, cos_t, sin_t):
    b, t, _ = p_q.shape
    n_c = p_ctx.shape[1]
    has_lat = p_lat is not None
    tq = min(t, 256)
    gw = ATT_GROUP * HEAD_DIM
    q_blk, k_blk, v_blk = 2048 // gw, (2048 + 1024) // HEAD_DIM, (2048 + 1024 + 256) // HEAD_DIM
    vec = pl.BlockSpec((1, HEAD_DIM), lambda bi, g, qi: (0, 0))
    in_specs = [pl.BlockSpec((1, tq, gw), lambda bi, g, qi: (bi, qi, q_blk + g)),
                pl.BlockSpec((1, n_c, HEAD_DIM), lambda bi, g, qi: (bi, 0, k_blk + g)),
                pl.BlockSpec((1, n_c, HEAD_DIM), lambda bi, g, qi: (bi, 0, v_blk + g))]
    args = [p_q, p_ctx, p_ctx]
    scratch = [pltpu.VMEM((n_c, HEAD_DIM), BF16)]
    if has_lat:
        n_l = p_lat.shape[1]
        in_specs += [pl.BlockSpec((1, n_l, HEAD_DIM), lambda bi, g, qi: (bi, 0, k_blk + g)),
                     pl.BlockSpec((1, n_l, HEAD_DIM), lambda bi, g, qi: (bi, 0, v_blk + g))]
        args += [p_lat, p_lat]
        scratch.append(pltpu.VMEM((n_l, HEAD_DIM), BF16))
    in_specs += [vec, vec]
    args += [q_norm.reshape(1, HEAD_DIM), k_norm.reshape(1, HEAD_DIM)]
    if has_lat:
        in_specs += [pl.BlockSpec((tq, HEAD_DIM), lambda bi, g, qi: (qi, 0)),
                     pl.BlockSpec((tq, HEAD_DIM), lambda bi, g, qi: (qi, 0)),
                     pl.BlockSpec((n_l, HEAD_DIM), lambda bi, g, qi: (0, 0)),
                     pl.BlockSpec((n_l, HEAD_DIM), lambda bi, g, qi: (0, 0))]
        args += [cos_t, sin_t, cos_t, sin_t]
    return pl.pallas_call(
        functools.partial(_attn_kernel, has_lat=has_lat),
        grid=(b, ATT_HEADS // ATT_GROUP, t // tq),
        in_specs=in_specs,
        out_specs=pl.BlockSpec((1, tq, gw), lambda bi, g, qi: (bi, qi, g)),
        out_shape=jax.ShapeDtypeStruct((b, t, ATT_HEADS * HEAD_DIM), BF16),
        scratch_shapes=scratch,
        compiler_params=_cparams("parallel", "parallel", "arbitrary"),
        name="attention_lat" if has_lat else "attention_ctx",
    )(*args)


def _conformer_kernel(x_ref, xp_ref, xn_ref, w_ref, b_ref, g_ref, be_ref, o_ref, xs, *, tt):
    i, n = pl.program_id(1), pl.num_programs(1)

    def glu(ref):
        v = ref[0].astype(F32)
        return v[:, :CONV_WIDTH] * jax.nn.sigmoid(v[:, CONV_WIDTH:])

    xs[0:HALO] = jnp.where(i > 0, glu(xp_ref), 0.0)
    xs[HALO:HALO + tt] = glu(x_ref)
    xs[HALO + tt:] = jnp.where(i < n - 1, glu(xn_ref), 0.0)
    pad = CONV_KERNEL // 2
    acc = jnp.zeros((tt, CONV_WIDTH), F32)
    for k in range(CONV_KERNEL):
        off = HALO - pad + k
        acc = acc + w_ref[k:k + 1, :] * xs[off:off + tt, :]
    h = acc + b_ref[...]
    hc = h - jnp.mean(h, axis=-1, keepdims=True)
    var = jnp.mean(hc * hc, axis=-1, keepdims=True)
    y = hc * lax.rsqrt(var + NORM_EPS) * g_ref[...] + be_ref[...]
    o_ref[0] = _silu(y).astype(o_ref.dtype)


def _conformer(p, conv_w, conv_b, norm_g, norm_b):
    b, t, _ = p.shape
    tt = min(t, 256)
    nh = t // HALO
    cw = CONV_WIDTH
    vec = pl.BlockSpec((1, cw), lambda bi, i: (0, 0))
    return pl.pallas_call(
        functools.partial(_conformer_kernel, tt=tt),
        grid=(b, t // tt),
        in_specs=[pl.BlockSpec((1, tt, 2 * cw), lambda bi, i: (bi, i, 0)),
                  pl.BlockSpec((1, HALO, 2 * cw), lambda bi, i: (bi, jnp.maximum(i * (tt // HALO) - 1, 0), 0)),
                  pl.BlockSpec((1, HALO, 2 * cw), lambda bi, i: (bi, jnp.minimum((i + 1) * (tt // HALO), nh - 1), 0)),
                  pl.BlockSpec((CONV_KERNEL, cw), lambda bi, i: (0, 0)),
                  vec, vec, vec],
        out_specs=pl.BlockSpec((1, tt, cw), lambda bi, i: (bi, i, 0)),
        out_shape=jax.ShapeDtypeStruct((b, t, cw), BF16),
        scratch_shapes=[pltpu.VMEM((tt + 2 * HALO, cw), F32)],
        compiler_params=_cparams("parallel", "parallel"),
        name="conformer",
    )(p, p, p, conv_w, conv_b.reshape(1, cw), norm_g.reshape(1, cw), norm_b.reshape(1, cw))


def _head_ones(n, width):
    r = lax.broadcasted_iota(jnp.int32, (n, n), 0) // width
    c = lax.broadcasted_iota(jnp.int32, (n, n), 1) // width
    return jnp.where(r == c, 1.0, 0.0).astype(BF16)


def _softplus(x):
    return jnp.maximum(x, 0.0) + jnp.log(1.0 + jnp.exp(-jnp.abs(x)))


def _rwkv_feat_kernel(z_ref, zp_ref, zn_ref, mu_ref, w0_ref, w2_ref, a0_ref, a2_ref, g2_ref, kk_ref, ka_ref,
                      r_o, v_o, kk_o, lw0_o, lw1_o, kd0_o, kd1_o, b0_o, b1_o, gate_o, zs, *, tt):
    i, n = pl.program_id(1), pl.num_programs(1)
    zs[0:HALO] = jnp.where(i > 0, zp_ref[0], 0.0)
    zs[HALO:HALO + tt] = z_ref[0]
    zs[HALO + tt:] = jnp.where(i < n - 1, zn_ref[0], 0.0)
    z = z_ref[0]
    z_prev = zs[HALO - 1:HALO - 1 + tt, :]
    z_next = zs[HALO + 1:HALO + 1 + tt, :]
    x = z + mu_ref[0:1, :] * (z_prev - z) + mu_ref[1:2, :] * (z_next - z)
    w = RWKV_WIDTH
    r, k, v = x[:, :w], x[:, w:2 * w], x[:, 2 * w:3 * w]
    lora = x[:, 3 * w:3 * w + LORA_IN]
    gate_in = x[:, 3 * w + LORA_IN:]
    r_o[0] = r
    v_o[0] = v
    kk = k * kk_ref[...]
    ones = _head_ones(LANES, RWKV_HEAD)
    sq = kk * kk
    ssum = jnp.concatenate([_split_dot(sq[:, c * LANES:(c + 1) * LANES], ones, 2) for c in range(w // LANES)], axis=-1)
    kk = kk * lax.rsqrt(ssum + 1e-12)
    kk_o[0] = kk
    lora_t = jnp.tanh(lora).astype(BF16)
    lora_b = lora.astype(BF16)
    for d, (lw_o, kd_o, b_o) in enumerate(((lw0_o, kd0_o, b0_o), (lw1_o, kd1_o, b1_o))):
        wl = -_softplus(-(w0_ref[d:d + 1, :] + _dot(lora_t, w2_ref[d]))) - 0.5
        lw_o[0] = -jnp.exp(wl)
        a = jax.nn.sigmoid(a0_ref[d:d + 1, :] + _dot(lora_b, a2_ref[d]))
        kd_o[0] = k * (1.0 + (a - 1.0) * ka_ref[...])
        b_o[0] = kk * a
    gate_o[0] = _dot(jax.nn.sigmoid(gate_in).astype(BF16), g2_ref[...])


def _rwkv_features(z, mu, w0, w2p, a0, a2p, g2p, k_k, k_a):
    b, t, zw = z.shape
    tt = min(t, 256)
    nh = t // HALO
    w = RWKV_WIDTH
    vec = pl.BlockSpec((1, w), lambda bi, i: (0, 0))
    out_spec = pl.BlockSpec((1, tt, w), lambda bi, i: (bi, i, 0))
    out = jax.ShapeDtypeStruct((b, t, w), F32)
    return pl.pallas_call(
        functools.partial(_rwkv_feat_kernel, tt=tt),
        grid=(b, t // tt),
        in_specs=[pl.BlockSpec((1, tt, zw), lambda bi, i: (bi, i, 0)),
                  pl.BlockSpec((1, HALO, zw), lambda bi, i: (bi, jnp.maximum(i * (tt // HALO) - 1, 0), 0)),
                  pl.BlockSpec((1, HALO, zw), lambda bi, i: (bi, jnp.minimum((i + 1) * (tt // HALO), nh - 1), 0)),
                  pl.BlockSpec((2, zw), lambda bi, i: (0, 0)),
                  pl.BlockSpec((2, w), lambda bi, i: (0, 0)),
                  pl.BlockSpec((2, LORA_IN, w), lambda bi, i: (0, 0, 0)),
                  pl.BlockSpec((2, w), lambda bi, i: (0, 0)),
                  pl.BlockSpec((2, LORA_IN, w), lambda bi, i: (0, 0, 0)),
                  pl.BlockSpec((GATE_PAD, w), lambda bi, i: (0, 0)),
                  vec, vec],
        out_specs=[out_spec] * 10,
        out_shape=[out] * 10,
        scratch_shapes=[pltpu.VMEM((tt + 2 * HALO, zw), F32)],
        compiler_params=_cparams("parallel", "parallel"),
        name="rwkv_features",
    )(z, z, z, mu, w0, w2p, a0, a2p, g2p, k_k.reshape(1, w), k_a.reshape(1, w))


def _rwkv_chunk_terms(chains, want_y):
    n = len(chains)
    rng = range(n)
    c = chains[0][1].shape[0]
    low_lane = chains[0][6][3]
    block_mask = chains[0][6][4]
    r, v, kk, lw, kd, beta = ([ch[k] for ch in chains] for k in range(6))
    tri_incl, incl2, strict2 = ([ch[6][k] for ch in chains] for k in range(3))

    def stack_heads(x):
        return jnp.concatenate([jnp.where(low_lane, x, 0.0), jnp.where(low_lane, 0.0, x)], axis=0)

    def unstack(x):
        return jnp.where(low_lane, x[:c], x[c:])

    cs = [_ones_dot(tri_incl[i], lw[i], 3) for i in rng]
    tot = [jnp.sum(lw[i], axis=0, keepdims=True) for i in rng]
    e_neg = [jnp.exp(-cs[i]) for i in rng]
    e_rem = [jnp.exp(tot[i] - cs[i]) for i in rng]
    kt_st = [stack_heads(kk[i] * jnp.exp(cs[i] - lw[i])) for i in rng]
    kt_sb = [kt_st[i].astype(BF16) for i in rng]
    bh_sb = [stack_heads(beta[i] * e_neg[i]).astype(BF16) for i in rng]
    kh_sb = [stack_heads(kd[i] * e_neg[i]).astype(BF16) for i in rng]
    right = [jnp.concatenate([beta[i] * e_rem[i], kd[i] * e_rem[i]], axis=0).astype(BF16) for i in rng]
    v_b = [v[i].astype(BF16) for i in rng]
    v2 = [jnp.concatenate([v_b[i], v_b[i]], axis=0) for i in rng]
    a_kb = [_dot_nt(kt_sb[i], bh_sb[i]) * strict2[i] for i in rng]
    a_kk = [(_dot_nt(kt_sb[i], kh_sb[i]) * strict2[i]).astype(BF16) for i in rng]
    if want_y:
        rt = [r[i] * jnp.exp(cs[i]) for i in rng]
        rt_sb = [stack_heads(rt[i]).astype(BF16) for i in rng]
        a_rb = [(_dot_nt(rt_sb[i], bh_sb[i]) * incl2[i]).astype(BF16) for i in rng]
        a_rk = [(_dot_nt(rt_sb[i], kh_sb[i]) * incl2[i]).astype(BF16) for i in rng]
    rhs0 = [_dot(a_kk[i], v2[i]) for i in rng]
    p = [-a_kb[i] for i in rng]
    x = list(p)
    for _ in range(int(np.log2(c)) - 1):
        p_b = [p[i].astype(BF16) for i in rng]
        x_b = [x[i].astype(BF16) for i in rng]
        p = [_dot(p_b[i], p_b[i]) for i in rng]
        xp = [_dot(x_b[i], p[i].astype(BF16)) for i in rng]
        x = [x[i] + p[i] + xp[i] for i in rng]
    both = [_dot(x[i].astype(BF16), jnp.concatenate([rhs0[i].astype(BF16), kt_sb[i]], axis=1)) for i in rng]
    u0 = [unstack(-(rhs0[i] + both[i][:, :LANES])) for i in rng]
    kq_st = [kt_st[i] + both[i][:, LANES:] for i in rng]
    kq = [unstack(kq_st[i]) for i in rng]
    lhs_t = [jnp.concatenate([jnp.concatenate([u0[i], v[i]], axis=0).T,
                              jnp.concatenate([kq[i], jnp.zeros_like(kq[i])], axis=0).T], axis=0).astype(BF16)
             for i in rng]
    gq = [_dot(lhs_t[i], right[i]) for i in rng]
    if want_y:
        u0_b = [u0[i].astype(BF16) for i in rng]
        y0 = [_dot(jnp.concatenate([a_rb[i], a_rk[i]], axis=1), jnp.concatenate([u0_b[i], u0_b[i], v2[i]], axis=0))
              for i in rng]
        rq = [_dot(a_rb[i], kq_st[i].astype(BF16)) for i in rng]
    out = []
    for i in rng:
        g_add = gq[i][:LANES] * block_mask
        q_mat = (gq[i][LANES:] * block_mask).astype(BF16)
        if want_y:
            out.append((jnp.exp(tot[i]), q_mat, g_add, (rt[i] - unstack(rq[i])).astype(BF16), unstack(y0[i])))
        else:
            out.append((jnp.exp(tot[i]), q_mat, g_add, None, None))
    return out


def _rwkv_consts(c, reverse):
    t = lax.broadcasted_iota(jnp.int32, (c, c), 0)
    s = lax.broadcasted_iota(jnp.int32, (c, c), 1)
    incl = (s >= t) if reverse else (s <= t)
    tri_incl = jnp.where(incl, 1.0, 0.0).astype(BF16)
    t2 = lax.broadcasted_iota(jnp.int32, (2 * c, 2 * c), 0)
    s2 = lax.broadcasted_iota(jnp.int32, (2 * c, 2 * c), 1)
    same = (t2 // c) == (s2 // c)
    order = (s2 >= t2) if reverse else (s2 <= t2)
    incl2 = jnp.where(same & order, 1.0, 0.0)
    strict2 = jnp.where(same & order & (s2 != t2), 1.0, 0.0)
    low_lane = lax.broadcasted_iota(jnp.int32, (1, LANES), 1) < RWKV_HEAD
    bi = lax.broadcasted_iota(jnp.int32, (LANES, LANES), 0) // RWKV_HEAD
    bj = lax.broadcasted_iota(jnp.int32, (LANES, LANES), 1) // RWKV_HEAD
    block_mask = jnp.where(bi == bj, 1.0, 0.0)
    return tri_incl, incl2, strict2, low_lane, block_mask


def _rwkv_kernel(r_l, v_l, kk_l, lw0_l, lw1_l, kd0_l, kd1_l, b0_l, b1_l, gate_l,
                 v_c, kk_c, lw0_c, lw1_c, kd0_c, kd1_c, b0_c, b1_c,
                 rk_ref, lng_ref, lnb_ref, o_ref, acc_s, gam_s, q_s, g_s, re_s):
    c = RWKV_CHUNK
    n_l, n_c = r_l.shape[1] // c, v_c.shape[1] // c
    consts = (_rwkv_consts(c, False), _rwkv_consts(c, True))
    lat_dirs = ((lw0_l, kd0_l, b0_l), (lw1_l, kd1_l, b1_l))
    ctx_dirs = ((lw0_c, kd0_c, b0_c), (lw1_c, kd1_c, b1_c))

    def terms_body(ii, carry, *, refs, dirs, slot0, want_y):
        r_ref, v_ref, kk_ref = refs
        chains, where = [], []
        for g in range(RWKV_GROUP):
            ci = ii * RWKV_GROUP + g
            rows = pl.ds(pl.multiple_of(ci * c, c), c)
            r = r_ref[0, rows, :] if want_y else None
            v, kk = v_ref[0, rows, :], kk_ref[0, rows, :]
            for d, (lw_r, kd_r, b_r) in enumerate(dirs):
                chains.append((r, v, kk, lw_r[0, rows, :], kd_r[0, rows, :], b_r[0, rows, :], consts[d]))
                where.append((d, ci, rows))
        terms = _rwkv_chunk_terms(chains, want_y)
        for (d, ci, rows), (gam, q_mat, g_add, r_eff, _) in zip(where, terms):
            gam_s[d, slot0 + ci] = gam
            q_s[d, slot0 + ci] = q_mat
            g_s[d, slot0 + ci] = g_add
            if want_y:
                re_s[d, ci] = r_eff
        if want_y:
            for g in range(RWKV_GROUP):
                acc_s[where[2 * g][2], :] = terms[2 * g][4] + terms[2 * g + 1][4]
        return carry

    lax.fori_loop(0, n_c // RWKV_GROUP,
                  functools.partial(terms_body, refs=(None, v_c, kk_c), dirs=ctx_dirs, slot0=0, want_y=False), 0)
    lax.fori_loop(0, n_l // RWKV_GROUP,
                  functools.partial(terms_body, refs=(r_l, v_l, kk_l), dirs=lat_dirs, slot0=n_c, want_y=True), 0)

    def advance(S, d, slot):
        s_b = S.astype(BF16)
        return S * gam_s[d, slot] - _dot(s_b, q_s[d, slot]) + g_s[d, slot], s_b

    def ctx_scan(k, carry):
        s_f, s_r = carry
        s_f, _ = advance(s_f, 0, k)
        s_r, _ = advance(s_r, 1, n_c - 1 - k)
        return s_f, s_r

    def lat_scan(k, carry):
        s_f, s_r = carry
        kr = n_l - 1 - k
        s_f, sf_b = advance(s_f, 0, n_c + k)
        s_r, sr_b = advance(s_r, 1, n_c + kr)
        rows_f = pl.ds(pl.multiple_of(k * c, c), c)
        rows_r = pl.ds(pl.multiple_of(kr * c, c), c)
        acc_s[rows_f, :] += _dot_nt(re_s[0, k], sf_b)
        acc_s[rows_r, :] += _dot_nt(re_s[1, kr], sr_b)
        return s_f, s_r

    zero = jnp.zeros((LANES, LANES), F32)
    carry = lax.fori_loop(0, n_c, ctx_scan, (zero, zero))
    lax.fori_loop(0, n_l, lat_scan, carry)

    ones = _head_ones(LANES, RWKV_HEAD)
    wkv = acc_s[...]
    inv_n = 1.0 / RWKV_HEAD
    mean = _split_dot(wkv, ones, 2) * inv_n
    xc = wkv - mean
    var = _split_dot(xc * xc, ones, 2) * inv_n
    y = xc * lax.rsqrt(var + RWKV_GN_EPS) * lng_ref[...] + lnb_ref[...]
    r = r_l[0]
    rk = rk_ref[...]
    bonus = (_split_dot(r * kd0_l[0] * rk, ones, 2) + _split_dot(r * kd1_l[0] * rk, ones, 2)) * v_l[0]
    o_ref[0] = ((y + bonus) * gate_l[0]).astype(o_ref.dtype)


def _rwkv_mix(feat_l, feat_c, r_k, ln_g, ln_b):
    r_l, v_l, kk_l, lw0_l, lw1_l, kd0_l, kd1_l, b0_l, b1_l, gate_l = feat_l
    _, v_c, kk_c, lw0_c, lw1_c, kd0_c, kd1_c, b0_c, b1_c, _ = feat_c
    b, t, w = r_l.shape
    t_c = v_c.shape[1]
    n_slots = (t_c + t) // RWKV_CHUNK
    lat = pl.BlockSpec((1, t, LANES), lambda bi, p: (bi, 0, p))
    ctx = pl.BlockSpec((1, t_c, LANES), lambda bi, p: (bi, 0, p))
    vec = pl.BlockSpec((1, LANES), lambda bi, p: (0, p))
    return pl.pallas_call(
        _rwkv_kernel,
        grid=(b, w // LANES),
        in_specs=[lat] * 10 + [ctx] * 8 + [vec] * 3,
        out_specs=lat,
        out_shape=jax.ShapeDtypeStruct((b, t, w), BF16),
        scratch_shapes=[pltpu.VMEM((t, LANES), F32),
                        pltpu.VMEM((2, n_slots, 1, LANES), F32),
                        pltpu.VMEM((2, n_slots, LANES, LANES), BF16),
                        pltpu.VMEM((2, n_slots, LANES, LANES), F32),
                        pltpu.VMEM((2, t // RWKV_CHUNK, RWKV_CHUNK, LANES), BF16)],
        compiler_params=_cparams("parallel", "parallel"),
        name="rwkv_mix",
    )(r_l, v_l, kk_l, lw0_l, lw1_l, kd0_l, kd1_l, b0_l, b1_l, gate_l,
      v_c, kk_c, lw0_c, lw1_c, kd0_c, kd1_c, b0_c, b1_c,
      r_k.reshape(1, w), ln_g.reshape(1, w), ln_b.reshape(1, w))


def _retention_kernel(dl_ref, q_l, k_l, v_l, rg_l, k_c, v_c, cos_ref, sin_ref, gg_ref, gb_ref, o_ref,
                      q_s, k_s, acc_s):
    c = RET_CHUNK
    n_l, n_c = q_l.shape[1] // c, k_c.shape[1] // c
    h = pl.program_id(1)
    k_scale = HEAD_DIM ** -0.5
    q_s[...] = _rope(q_l[0].astype(F32), cos_ref[...], sin_ref[...]).astype(BF16)
    k_s[...] = (_rope(k_l[0].astype(F32), cos_ref[...], sin_ref[...]) * k_scale).astype(BF16)
    ti = lax.broadcasted_iota(jnp.int32, (c, c), 0).astype(F32)
    si = lax.broadcasted_iota(jnp.int32, (c, c), 1).astype(F32)
    idx = lax.broadcasted_iota(jnp.int32, (c, 1), 0).astype(F32)
    for d in range(2):
        reverse = d == 1
        lg_row = jax.nn.log_sigmoid(jnp.full((1, LANES), dl_ref[d, h], F32))
        lg = lg_row[:, :1]
        dist = (si - ti) if reverse else (ti - si)
        inner = jnp.where(dist >= 0, jnp.exp(lg * jnp.maximum(dist, 0.0)), 0.0)
        pos = (c - 1.0 - idx) if reverse else idx
        q_decay = jnp.exp(lg * (pos + 1.0))
        k_decay = jnp.exp(lg * (c - 1.0 - pos))
        chunk_decay = jnp.exp(lg * c)

        def state_update(R, k_f32, v_b):
            kd_t = (k_f32 * k_decay).T.astype(BF16)
            return R * chunk_decay + _dot(kd_t, v_b)

        def ctx_body(ci, R):
            cc = (n_c - 1 - ci) if reverse else ci
            rows = pl.ds(pl.multiple_of(cc * c, c), c)
            return state_update(R, k_c[0, rows, :].astype(F32) * k_scale, v_c[0, rows, :])

        def lat_body(ci, R):
            cc = (n_l - 1 - ci) if reverse else ci
            rows = pl.ds(pl.multiple_of(cc * c, c), c)
            q_b, k_b, v_b = q_s[rows, :], k_s[rows, :], v_l[0, rows, :]
            s = _dot_nt(q_b, k_b) * inner
            o = _dot(s.astype(BF16), v_b) + _dot(q_b, R.astype(BF16)) * q_decay
            if reverse:
                acc_s[rows, :] += o
            else:
                acc_s[rows, :] = o
            return state_update(R, k_b.astype(F32), v_b)

        R = lax.fori_loop(0, n_c, ctx_body, jnp.zeros((HEAD_DIM, RET_V_DIM), F32))
        lax.fori_loop(0, n_l, lat_body, R)

    x = acc_s[...]
    xc = x - jnp.mean(x, axis=-1, keepdims=True)
    var = jnp.mean(xc * xc, axis=-1, keepdims=True)
    y = xc * lax.rsqrt(var + NORM_EPS) * gg_ref[...] + gb_ref[...]
    o_ref[0] = (y * _silu(rg_l[0].astype(F32))).astype(o_ref.dtype)


def _retention(p_l, p_c, decay_logit, gn_g, gn_b, cos_t, sin_t):
    b, t, _ = p_l.shape
    t_c = p_c.shape[1]
    nh, dk, dv = RET_HEADS, HEAD_DIM, RET_V_DIM
    k_blk, v_blk, g_blk = nh, (2 * nh * dk) // dv, (2 * nh * dk + nh * dv) // dv
    return pl.pallas_call(
        _retention_kernel,
        grid=(b, nh),
        in_specs=[pl.BlockSpec(memory_space=pltpu.SMEM),
                  pl.BlockSpec((1, t, dk), lambda bi, h: (bi, 0, h)),
                  pl.BlockSpec((1, t, dk), lambda bi, h: (bi, 0, k_blk + h)),
                  pl.BlockSpec((1, t, dv), lambda bi, h: (bi, 0, v_blk + h)),
                  pl.BlockSpec((1, t, dv), lambda bi, h: (bi, 0, g_blk + h)),
                  pl.BlockSpec((1, t_c, dk), lambda bi, h: (bi, 0, k_blk + h)),
                  pl.BlockSpec((1, t_c, dv), lambda bi, h: (bi, 0, v_blk + h)),
                  pl.BlockSpec((t, dk), lambda bi, h: (0, 0)),
                  pl.BlockSpec((t, dk), lambda bi, h: (0, 0)),
                  pl.BlockSpec((1, dv), lambda bi, h: (0, h)),
                  pl.BlockSpec((1, dv), lambda bi, h: (0, h))],
        out_specs=pl.BlockSpec((1, t, dv), lambda bi, h: (bi, 0, h)),
        out_shape=jax.ShapeDtypeStruct((b, t, nh * dv), BF16),
        scratch_shapes=[pltpu.VMEM((t, dk), BF16), pltpu.VMEM((t, dk), BF16), pltpu.VMEM((t, dv), F32)],
        compiler_params=_cparams("parallel", "parallel"),
        name="retention",
    )(decay_logit, p_l, p_l, p_l, p_l, p_c, p_c, cos_t, sin_t, gn_g.reshape(1, nh * dv), gn_b.reshape(1, nh * dv))


def kernel(x, c, ctx, c_ctx, mod_w, mod_b, norm_mix_g, norm_ffn_g, ffn_w_up, ffn_conv_w, ffn_conv_b, ffn_w_down, ab_w_in, ab_q_norm, ab_k_norm, ab_conv_w, ab_conv_b, ab_conv_norm_g, ab_conv_norm_b, ab_w_out, cd_w_in, cd_shift_mu, rwkv_w0, rwkv_w2, rwkv_a0, rwkv_a2, rwkv_g2, rwkv_k_k, rwkv_k_a, rwkv_r_k, rwkv_ln_g, rwkv_ln_b, ret_decay_logit, ret_gn_g, ret_gn_b, cd_w_out, final_norm_g):
    b, t, d = x.shape
    t_c = ctx.shape[1]
    att_w = ATT_HEADS * HEAD_DIM
    kv_w = att_w // ATT_GROUP

    pad_rows = (-(b + 1)) % 8
    c_all = jnp.concatenate([c, c_ctx[None, :], jnp.zeros((pad_rows, d), F32)], axis=0)
    mod = _modulation(c_all, mod_w, mod_b)

    def mod_vectors(layer):
        lat = [m.reshape(b, 1, d) for m in jnp.split(mod[layer, :b], 6, axis=-1)]
        cx = [m.reshape(1, 1, d) for m in jnp.split(mod[layer, b:b + 1], 6, axis=-1)]
        return lat, cx

    cos_t, sin_t = _rope_tables(t)
    h = x
    hc = ctx.reshape(1, b * t_c, d)

    (sh1, sc1, g1, sh2, sc2, g2), (csh1, csc1, cg1, csh2, csc2, cg2) = mod_vectors(0)
    w_in = ab_w_in[0]
    glu0 = att_w + 2 * kv_w
    w_in = jnp.concatenate([w_in[:, glu0:], w_in[:, :glu0]], axis=1).astype(BF16)
    p_l = _norm_mod_matmul(h, norm_mix_g[0], sh1, sc1, w_in, BF16, "ab_in_lat")
    p_c = _norm_mod_matmul(hc, norm_mix_g[0], csh1, csc1, w_in, BF16, "ab_in_ctx").reshape(b, t_c, -1)
    att_l = _attention(p_l, p_c, p_l, ab_q_norm[0], ab_k_norm[0], cos_t, sin_t)
    att_c = _attention(p_c, p_c, None, ab_q_norm[0], ab_k_norm[0], None, None)
    cv_l = _conformer(p_l, ab_conv_w[0], ab_conv_b[0], ab_conv_norm_g[0], ab_conv_norm_b[0])
    cv_c = _conformer(p_c, ab_conv_w[0], ab_conv_b[0], ab_conv_norm_g[0], ab_conv_norm_b[0])
    w_out = ab_w_out[0].astype(BF16)
    w_o = [w_out[:att_w], w_out[att_w:]]
    h = _matmul_gated_residual([att_l, cv_l], w_o, h, g1, "ab_out_lat")
    hc = _matmul_gated_residual([att_c.reshape(1, b * t_c, -1), cv_c.reshape(1, b * t_c, -1)], w_o, hc, cg1,
                                "ab_out_ctx")
    w_up, w_down = ffn_w_up[0].astype(BF16), ffn_w_down[0].astype(BF16)
    h = _conv_ffn(h, norm_ffn_g[0], sh2, sc2, g2, w_up, ffn_conv_w[0], ffn_conv_b[0], w_down, final_norm_g,
                  t, False, "ffn0_lat")
    hc = _conv_ffn(hc, norm_ffn_g[0], csh2, csc2, cg2, w_up, ffn_conv_w[0], ffn_conv_b[0], w_down,
                   final_norm_g, t_c, False, "ffn0_ctx")

    (sh1, sc1, g1, sh2, sc2, g2), (csh1, csc1, _, _, _, _) = mod_vectors(1)
    rw_in = 3 * RWKV_WIDTH + 2 * RWKV_HEAD + GATE_LORA
    zpad = RWKV_IN_PAD - rw_in
    w_z = jnp.pad(cd_w_in[0][:, :rw_in], ((0, 0), (0, zpad))).astype(BF16)
    w_ret = cd_w_in[0][:, rw_in:].astype(BF16)
    z_l = _norm_mod_matmul(h, norm_mix_g[1], sh1, sc1, w_z, F32, "cd_in_z_lat")
    z_c = _norm_mod_matmul(hc, norm_mix_g[1], csh1, csc1, w_z, F32, "cd_in_z_ctx").reshape(b, t_c, -1)
    rp_l = _norm_mod_matmul(h, norm_mix_g[1], sh1, sc1, w_ret, BF16, "cd_in_ret_lat")
    rp_c = _norm_mod_matmul(hc, norm_mix_g[1], csh1, csc1, w_ret, BF16, "cd_in_ret_ctx").reshape(b, t_c, -1)

    mu = jnp.pad(cd_shift_mu[0], ((0, 0), (0, zpad)))
    lora_pad = LORA_IN - RWKV_HEAD
    w2p = jnp.pad(rwkv_w2[0], ((0, 0), (0, lora_pad), (0, 0))).astype(BF16)
    a2p = jnp.pad(rwkv_a2[0], ((0, 0), (lora_pad, 0), (0, 0))).astype(BF16)
    g2p = jnp.pad(rwkv_g2[0], ((0, GATE_PAD - GATE_LORA), (0, 0))).astype(BF16)
    feat_l = _rwkv_features(z_l, mu, rwkv_w0[0], w2p, rwkv_a0[0], a2p, g2p, rwkv_k_k[0], rwkv_k_a[0])
    feat_c = _rwkv_features(z_c, mu, rwkv_w0[0], w2p, rwkv_a0[0], a2p, g2p, rwkv_k_k[0], rwkv_k_a[0])
    y_c = _rwkv_mix(feat_l, feat_c, rwkv_r_k[0], rwkv_ln_g[0], rwkv_ln_b[0])
    y_d = _retention(rp_l, rp_c, ret_decay_logit[0], ret_gn_g[0], ret_gn_b[0], cos_t, sin_t)
    w_out = cd_w_out[0].astype(BF16)
    h = _matmul_gated_residual([y_c, y_d], [w_out[:RWKV_WIDTH], w_out[RWKV_WIDTH:]], h, g1, "cd_out_lat")
    w_up, w_down = ffn_w_up[1].astype(BF16), ffn_w_down[1].astype(BF16)
    return _conv_ffn(h, norm_ffn_g[1], sh2, sc2, g2, w_up, ffn_conv_w[1], ffn_conv_b[1], w_down, final_norm_g,
                     t, True, "ffn1_lat")
```

```python
import functools

import jax
import jax.numpy as jnp
import numpy as np
from jax import lax
from jax.experimental import pallas as pl
from jax.experimental.pallas import tpu as pltpu

F32 = jnp.float32
BF16 = jnp.bfloat16

NORM_EPS = 1e-6
GRID_W = 64
ROPE_BASE = 10000.0
HEAD_DIM = 128
ATT_HEADS = 8
ATT_GROUP = 4
CONV_WIDTH = 1024
CONV_KERNEL = 31
RWKV_WIDTH = 1024
RWKV_HEAD = 64
RWKV_GN_EPS = 64e-5
RWKV_CHUNK = 64
RWKV_GROUP = 8
LORA_IN = 128
GATE_LORA = 160
GATE_PAD = 384
RWKV_IN_PAD = 3584
RET_HEADS = 8
RET_V_DIM = 256
RET_CHUNK = 128
RET_GROUP = 4
FFN_DIM = 5632
HALO = 16
NORM_PIECE = 16
INPROJ_ROW_CHUNK = 256
LANES = 128
SUBLANES = 8
LOG2_E = 1.4426950408889634

VMEM_LIMIT_BYTES = 52 * 1024 * 1024


def _cparams(*sem):
    return pltpu.CompilerParams(dimension_semantics=sem, vmem_limit_bytes=VMEM_LIMIT_BYTES)


def _dot(a, b):
    return jnp.dot(a, b, preferred_element_type=F32)


def _dot_nt(a, b):
    return lax.dot_general(a, b, (((1,), (1,)), ((), ())), preferred_element_type=F32)


def _split_dot(x, ones_bf16, terms):
    acc = None
    rem = x
    for _ in range(terms):
        piece = rem.astype(BF16)
        part = _dot(piece, ones_bf16)
        acc = part if acc is None else acc + part
        rem = rem - piece.astype(F32)
    return acc


def _ones_dot(ones_bf16, x, terms):
    acc = None
    rem = x
    for _ in range(terms):
        piece = rem.astype(BF16)
        part = _dot(ones_bf16, piece)
        acc = part if acc is None else acc + part
        rem = rem - piece.astype(F32)
    return acc


def _rms(x, g):
    return x * lax.rsqrt(jnp.mean(x * x, axis=-1, keepdims=True) + NORM_EPS) * g


def _silu(x):
    return x * jax.nn.sigmoid(x)


def _rope(x, cos, sin_signed):
    n = x.shape[-1]
    lane = lax.broadcasted_iota(jnp.int32, x.shape, x.ndim - 1)
    partner = jnp.where((lane % 64) < 32, pltpu.roll(x, n - 32, x.ndim - 1), pltpu.roll(x, 32, x.ndim - 1))
    return x * cos + partner * sin_signed


def _rope_tables(n_tok):
    rows = n_tok // GRID_W
    row = jnp.repeat(jnp.arange(rows, dtype=F32), GRID_W)
    col = jnp.tile(jnp.arange(GRID_W, dtype=F32), rows)
    axis_dim = HEAD_DIM // 2
    inv_freq = ROPE_BASE ** (-jnp.arange(0, axis_dim, 2, dtype=F32) / axis_dim)
    ang = jnp.concatenate([row[:, None] * inv_freq, col[:, None] * inv_freq], axis=-1)
    cos, sin = jnp.cos(ang), jnp.sin(ang)
    c_r, c_c, s_r, s_c = cos[:, :32], cos[:, 32:], sin[:, :32], sin[:, 32:]
    cos_t = jnp.concatenate([c_r, c_r, c_c, c_c], axis=-1)
    sin_t = jnp.concatenate([-s_r, s_r, -s_c, s_c], axis=-1)
    return cos_t, sin_t


def _mod_kernel(c_ref, w_ref, b_ref, o_ref):
    s = _silu(c_ref[...]).astype(BF16)
    o_ref[0] = _dot(s, w_ref[0].astype(BF16)) + b_ref[0]


def _modulation(c_all, mod_w, mod_b):
    n_layers, d, n = mod_w.shape
    rows = c_all.shape[0]
    tn = 1024
    return pl.pallas_call(
        _mod_kernel,
        grid=(n_layers, n // tn),
        in_specs=[pl.BlockSpec((rows, d), lambda l, j: (0, 0)),
                  pl.BlockSpec((1, d, tn), lambda l, j: (l, 0, j)),
                  pl.BlockSpec((1, 1, tn), lambda l, j: (l, 0, j))],
        out_specs=pl.BlockSpec((1, rows, tn), lambda l, j: (l, 0, j)),
        out_shape=jax.ShapeDtypeStruct((n_layers, rows, n), F32),
        compiler_params=_cparams("parallel", "parallel"),
        name="modulation",
    )(c_all, mod_w, mod_b.reshape(n_layers, 1, n))


def _norm_mod_rows(src, dst, dst_row0, n_rows, gain, shift):
    for r0 in range(0, n_rows, NORM_PIECE):
        x = src(r0)
        inv = lax.rsqrt(jnp.mean(x * x, axis=-1, keepdims=True) + NORM_EPS)
        dst[dst_row0 + r0:dst_row0 + r0 + NORM_PIECE, :] = (x * inv * gain + shift).astype(dst.dtype)


def _inproj_kernel(h_ref, g_ref, sh_ref, sc_ref, w_ref, o_ref, u_s, *, tm):
    j = pl.program_id(2)

    @pl.when(j == 0)
    def _():
        gain = g_ref[...] * (1.0 + sc_ref[0])
        shift = sh_ref[0]
        chunk = min(tm, INPROJ_ROW_CHUNK)
        for c0 in range(0, tm, chunk):
            _norm_mod_rows(lambda r0: h_ref[0, c0 + r0:c0 + r0 + NORM_PIECE, :], u_s, c0, chunk, gain, shift)
            o_ref[0, c0:c0 + chunk, :] = _dot(u_s[c0:c0 + chunk, :], w_ref[...]).astype(o_ref.dtype)

    @pl.when(j > 0)
    def _():
        o_ref[0] = _dot(u_s[...], w_ref[...]).astype(o_ref.dtype)


def _norm_mod_matmul(h, g, shift, scale, w, out_dtype, name):
    bm, t, d = h.shape
    n = w.shape[1]
    tm = min(t, 1024)
    tn = 512
    return pl.pallas_call(
        functools.partial(_inproj_kernel, tm=tm),
        grid=(bm, t // tm, n // tn),
        in_specs=[pl.BlockSpec((1, tm, d), lambda b, i, j: (b, i, 0)),
                  pl.BlockSpec((1, d), lambda b, i, j: (0, 0)),
                  pl.BlockSpec((1, 1, d), lambda b, i, j: (b, 0, 0)),
                  pl.BlockSpec((1, 1, d), lambda b, i, j: (b, 0, 0)),
                  pl.BlockSpec((d, tn), lambda b, i, j: (0, j))],
        out_specs=pl.BlockSpec((1, tm, tn), lambda b, i, j: (b, i, j)),
        out_shape=jax.ShapeDtypeStruct((bm, t, n), out_dtype),
        scratch_shapes=[pltpu.VMEM((tm, d), BF16)],
        compiler_params=_cparams("parallel", "parallel", "arbitrary"),
        name=name,
    )(h, g.reshape(1, d), shift, scale, w)


def _outproj_kernel(*refs, n_lhs):
    lhs, ws = refs[:n_lhs], refs[n_lhs:2 * n_lhs]
    res_ref, gt_ref, o_ref = refs[2 * n_lhs:]
    acc = _dot(lhs[0][0], ws[0][...])
    for a, w in zip(lhs[1:], ws[1:]):
        acc = acc + _dot(a[0], w[...])
    o_ref[0] = res_ref[0] + gt_ref[0] * acc


def _matmul_gated_residual(lhs_list, w_list, res, gate, name):
    bm, t, d = res.shape
    tm = min(t, 1024)
    tn = 512
    in_specs = [pl.BlockSpec((1, tm, a.shape[-1]), lambda b, i, j: (b, i, 0)) for a in lhs_list]
    in_specs += [pl.BlockSpec((w.shape[0], tn), lambda b, i, j: (0, j)) for w in w_list]
    in_specs += [pl.BlockSpec((1, tm, tn), lambda b, i, j: (b, i, j)),
                 pl.BlockSpec((1, 1, tn), lambda b, i, j: (b, 0, j))]
    return pl.pallas_call(
        functools.partial(_outproj_kernel, n_lhs=len(lhs_list)),
        grid=(bm, t // tm, d // tn),
        in_specs=in_specs,
        out_specs=pl.BlockSpec((1, tm, tn), lambda b, i, j: (b, i, j)),
        out_shape=jax.ShapeDtypeStruct((bm, t, d), F32),
        compiler_params=_cparams("parallel", "parallel", "arbitrary"),
        name=name,
    )(*lhs_list, *w_list, res, gate)


def _ffn_kernel(h_ref, hp_ref, hn_ref, g_ref, sh_ref, sc_ref, gt_ref, wg_ref, wv_ref, cw_ref, cb_ref,
                wd_ref, fg_ref, o_ref, u_s, *, tm, seq_len, final_norm):
    i, j, nj = pl.program_id(1), pl.program_id(2), pl.num_programs(2)

    @pl.when(j == 0)
    def _():
        gain = g_ref[...] * (1.0 + sc_ref[0])
        shift = sh_ref[0]
        _norm_mod_rows(lambda r0: hp_ref[0, r0:r0 + NORM_PIECE, :], u_s, 0, HALO, gain, shift)
        _norm_mod_rows(lambda r0: h_ref[0, r0:r0 + NORM_PIECE, :], u_s, HALO, tm, gain, shift)
        _norm_mod_rows(lambda r0: hn_ref[0, r0:r0 + NORM_PIECE, :], u_s, HALO + tm, HALO, gain, shift)

    rows = tm + 2 * HALO
    gate_lin = _dot(u_s[...], wg_ref[...])
    val = _dot(u_s[HALO:HALO + tm], wv_ref[...])
    pos = (i * tm + lax.broadcasted_iota(jnp.int32, (tm, 1), 0)) % seq_len
    g_prev = jnp.where(pos != 0, pltpu.roll(gate_lin, 1, 0)[HALO:HALO + tm], 0.0)
    g_next = jnp.where(pos != seq_len - 1, pltpu.roll(gate_lin, rows - 1, 0)[HALO:HALO + tm], 0.0)
    g_cur = gate_lin[HALO:HALO + tm]
    cw = cw_ref[...]
    conv = cw[0:1] * g_prev + cw[1:2] * g_cur + cw[2:3] * g_next + cb_ref[...]
    part = _dot((_silu(conv) * val).astype(BF16), wd_ref[...])

    @pl.when(j == 0)
    def _():
        o_ref[0] = part

    @pl.when(j > 0)
    def _():
        o_ref[0] += part

    @pl.when(j == nj - 1)
    def _():
        y = h_ref[0] + gt_ref[0] * o_ref[0]
        if final_norm:
            y = _rms(y, fg_ref[...])
        o_ref[0] = y


def _conv_ffn(h, g, shift, scale, gate, w_up, conv_w, conv_b, w_down, final_g, seq_len, final_norm, name):
    bm, t, d = h.shape
    f = w_down.shape[0]
    tm = min(t, 512)
    tf = 512
    nh = t // HALO
    kern = functools.partial(_ffn_kernel, tm=tm, seq_len=seq_len, final_norm=final_norm)
    vec = pl.BlockSpec((1, 1, d), lambda b, i, j: (b, 0, 0))
    return pl.pallas_call(
        kern,
        grid=(bm, t // tm, f // tf),
        in_specs=[pl.BlockSpec((1, tm, d), lambda b, i, j: (b, i, 0)),
                  pl.BlockSpec((1, HALO, d), lambda b, i, j: (b, jnp.maximum(i * (tm // HALO) - 1, 0), 0)),
                  pl.BlockSpec((1, HALO, d), lambda b, i, j: (b, jnp.minimum((i + 1) * (tm // HALO), nh - 1), 0)),
                  pl.BlockSpec((1, d), lambda b, i, j: (0, 0)),
                  vec, vec, vec,
                  pl.BlockSpec((d, tf), lambda b, i, j: (0, j)),
                  pl.BlockSpec((d, tf), lambda b, i, j: (0, f // tf + j)),
                  pl.BlockSpec((3, tf), lambda b, i, j: (0, j)),
                  pl.BlockSpec((1, tf), lambda b, i, j: (0, j)),
                  pl.BlockSpec((tf, d), lambda b, i, j: (j, 0)),
                  pl.BlockSpec((1, d), lambda b, i, j: (0, 0))],
        out_specs=pl.BlockSpec((1, tm, d), lambda b, i, j: (b, i, 0)),
        out_shape=jax.ShapeDtypeStruct((bm, t, d), F32),
        scratch_shapes=[pltpu.VMEM((tm + 2 * HALO, d), BF16)],
        compiler_params=_cparams("parallel", "parallel", "arbitrary"),
        name=name,
    )(h, h, h, g.reshape(1, d), shift, scale, gate, w_up, w_up, conv_w, conv_b.reshape(1, f), w_down,
      final_g.reshape(1, d))


def _attn_kernel(*refs, has_lat):
    if has_lat:
        (q_ref, kc_ref, vc_ref, kl_ref, vl_ref, qn_ref, kn_ref, cq_ref, sq_ref, ck_ref, sk_ref,
         o_ref, kc_s, kl_s) = refs
    else:
        q_ref, kc_ref, vc_ref, qn_ref, kn_ref, o_ref, kc_s = refs

    @pl.when(pl.program_id(2) == 0)
    def _():
        kc_s[...] = _rms(kc_ref[0].astype(F32), kn_ref[...]).astype(BF16)
        if has_lat:
            kl = _rms(kl_ref[0].astype(F32), kn_ref[...])
            kl_s[...] = _rope(kl, ck_ref[...], sk_ref[...]).astype(BF16)

    scale = HEAD_DIM ** -0.5 * LOG2_E
    for hh in range(ATT_GROUP):
        cols = slice(hh * HEAD_DIM, (hh + 1) * HEAD_DIM)
        q = _rms(q_ref[0, :, cols].astype(F32), qn_ref[...])
        if has_lat:
            q = _rope(q, cq_ref[...], sq_ref[...])
        q = (q * scale).astype(BF16)
        s_c = _dot_nt(q, kc_s[...])
        m = jnp.max(s_c, axis=-1, keepdims=True)
        if has_lat:
            s_l = _dot_nt(q, kl_s[...])
            m = jnp.maximum(m, jnp.max(s_l, axis=-1, keepdims=True))
        p_c = jnp.exp2(s_c - m)
        denom = jnp.sum(p_c, axis=-1, keepdims=True)
        acc = _dot(p_c.astype(BF16), vc_ref[0])
        if has_lat:
            p_l = jnp.exp2(s_l - m)
            denom = denom + jnp.sum(p_l, axis=-1, keepdims=True)
            acc = acc + _dot(p_l.astype(BF16), vl_ref[0])
        o_ref[0, :, cols] = (acc / denom).astype(o_ref.dtype)


def _attention(p_q, p_ctx, p_lat, q_norm, k_norm, cos_t, sin_t):
    b, t, _ = p_q.shape
    n_c = p_ctx.shape[1]
    has_lat = p_lat is not None
    tq = min(t, 256)
    gw = ATT_GROUP * HEAD_DIM
    q_blk, k_blk, v_blk = 2048 // gw, (2048 + 1024) // HEAD_DIM, (2048 + 1024 + 256) // HEAD_DIM
    vec = pl.BlockSpec((1, HEAD_DIM), lambda bi, g, qi: (0, 0))
    in_specs = [pl.BlockSpec((1, tq, gw), lambda bi, g, qi: (bi, qi, q_blk + g)),
                pl.BlockSpec((1, n_c, HEAD_DIM), lambda bi, g, qi: (bi, 0, k_blk + g)),
                pl.BlockSpec((1, n_c, HEAD_DIM), lambda bi, g, qi: (bi, 0, v_blk + g))]
    args = [p_q, p_ctx, p_ctx]
    scratch = [pltpu.VMEM((n_c, HEAD_DIM), BF16)]
    if has_lat:
        n_l = p_lat.shape[1]
        in_specs += [pl.BlockSpec((1, n_l, HEAD_DIM), lambda bi, g, qi: (bi, 0, k_blk + g)),
                     pl.BlockSpec((1, n_l, HEAD_DIM), lambda bi, g, qi: (bi, 0, v_blk + g))]
        args += [p_lat, p_lat]
        scratch.append(pltpu.VMEM((n_l, HEAD_DIM), BF16))
    in_specs += [vec, vec]
    args += [q_norm.reshape(1, HEAD_DIM), k_norm.reshape(1, HEAD_DIM)]
    if has_lat:
        in_specs += [pl.BlockSpec((tq, HEAD_DIM), lambda bi, g, qi: (qi, 0)),
                     pl.BlockSpec((tq, HEAD_DIM), lambda bi, g, qi: (qi, 0)),
                     pl.BlockSpec((n_l, HEAD_DIM), lambda bi, g, qi: (0, 0)),
                     pl.BlockSpec((n_l, HEAD_DIM), lambda bi, g, qi: (0, 0))]
        args += [cos_t, sin_t, cos_t, sin_t]
    return pl.pallas_call(
        functools.partial(_attn_kernel, has_lat=has_lat),
        grid=(b, ATT_HEADS // ATT_GROUP, t // tq),
        in_specs=in_specs,
        out_specs=pl.BlockSpec((1, tq, gw), lambda bi, g, qi: (bi, qi, g)),
        out_shape=jax.ShapeDtypeStruct((b, t, ATT_HEADS * HEAD_DIM), BF16),
        scratch_shapes=scratch,
        compiler_params=_cparams("parallel", "parallel", "arbitrary"),
        name="attention_lat" if has_lat else "attention_ctx",
    )(*args)


def _conformer_kernel(x_ref, xp_ref, xn_ref, w_ref, b_ref, g_ref, be_ref, o_ref, xs, *, tt):
    i, n = pl.program_id(1), pl.num_programs(1)

    def glu(ref):
        v = ref[0].astype(F32)
        return v[:, :CONV_WIDTH] * jax.nn.sigmoid(v[:, CONV_WIDTH:])

    xs[0:HALO] = jnp.where(i > 0, glu(xp_ref), 0.0)
    xs[HALO:HALO + tt] = glu(x_ref)
    xs[HALO + tt:] = jnp.where(i < n - 1, glu(xn_ref), 0.0)
    pad = CONV_KERNEL // 2
    offs = [HALO - pad + k for k in range(CONV_KERNEL)]
    window = xs[...]
    rows = tt + 2 * HALO
    acc = None
    for phase in range(SUBLANES):
        group = [o for o in offs if o % SUBLANES == phase]
        if not group:
            continue
        base = window if phase == 0 else pltpu.roll(window, rows - phase, 0)
        for o in group:
            k = o - (HALO - pad)
            term = w_ref[k:k + 1, :] * base[o - phase:o - phase + tt, :]
            acc = term if acc is None else acc + term
    h = acc + b_ref[...]
    hc = h - jnp.mean(h, axis=-1, keepdims=True)
    var = jnp.mean(hc * hc, axis=-1, keepdims=True)
    y = hc * lax.rsqrt(var + NORM_EPS) * g_ref[...] + be_ref[...]
    o_ref[0] = _silu(y).astype(o_ref.dtype)


def _conformer(p, conv_w, conv_b, norm_g, norm_b):
    b, t, _ = p.shape
    tt = min(t, 256)
    nh = t // HALO
    cw = CONV_WIDTH
    vec = pl.BlockSpec((1, cw), lambda bi, i: (0, 0))
    return pl.pallas_call(
        functools.partial(_conformer_kernel, tt=tt),
        grid=(b, t // tt),
        in_specs=[pl.BlockSpec((1, tt, 2 * cw), lambda bi, i: (bi, i, 0)),
                  pl.BlockSpec((1, HALO, 2 * cw), lambda bi, i: (bi, jnp.maximum(i * (tt // HALO) - 1, 0), 0)),
                  pl.BlockSpec((1, HALO, 2 * cw), lambda bi, i: (bi, jnp.minimum((i + 1) * (tt // HALO), nh - 1), 0)),
                  pl.BlockSpec((CONV_KERNEL, cw), lambda bi, i: (0, 0)),
                  vec, vec, vec],
        out_specs=pl.BlockSpec((1, tt, cw), lambda bi, i: (bi, i, 0)),
        out_shape=jax.ShapeDtypeStruct((b, t, cw), BF16),
        scratch_shapes=[pltpu.VMEM((tt + 2 * HALO, cw), F32)],
        compiler_params=_cparams("parallel", "parallel"),
        name="conformer",
    )(p, p, p, conv_w, conv_b.reshape(1, cw), norm_g.reshape(1, cw), norm_b.reshape(1, cw))


def _head_ones(n, width):
    r = lax.broadcasted_iota(jnp.int32, (n, n), 0) // width
    c = lax.broadcasted_iota(jnp.int32, (n, n), 1) // width
    return jnp.where(r == c, 1.0, 0.0).astype(BF16)


def _softplus(x):
    return jnp.maximum(x, 0.0) + jnp.log(1.0 + jnp.exp(-jnp.abs(x)))


def _rwkv_feat_kernel(z_ref, zp_ref, zn_ref, mu_ref, w0_ref, w2_ref, a0_ref, a2_ref, g2_ref, kk_ref, ka_ref,
                      r_o, v_o, kk_o, lw0_o, lw1_o, kd0_o, kd1_o, b0_o, b1_o, gate_o, zs, *, tt):
    i, n = pl.program_id(1), pl.num_programs(1)
    zs[0:HALO] = jnp.where(i > 0, zp_ref[0], 0.0)
    zs[HALO:HALO + tt] = z_ref[0]
    zs[HALO + tt:] = jnp.where(i < n - 1, zn_ref[0], 0.0)
    z = z_ref[0]
    z_prev = zs[HALO - 1:HALO - 1 + tt, :]
    z_next = zs[HALO + 1:HALO + 1 + tt, :]
    x = z + mu_ref[0:1, :] * (z_prev - z) + mu_ref[1:2, :] * (z_next - z)
    w = RWKV_WIDTH
    r, k, v = x[:, :w], x[:, w:2 * w], x[:, 2 * w:3 * w]
    lora = x[:, 3 * w:3 * w + LORA_IN]
    gate_in = x[:, 3 * w + LORA_IN:]
    r_o[0] = r
    v_o[0] = v
    kk = k * kk_ref[...]
    ones = _head_ones(LANES, RWKV_HEAD)
    sq = kk * kk
    ssum = jnp.concatenate([_split_dot(sq[:, c * LANES:(c + 1) * LANES], ones, 2) for c in range(w // LANES)], axis=-1)
    kk = kk * lax.rsqrt(ssum + 1e-12)
    kk_o[0] = kk
    lora_t = jnp.tanh(lora).astype(BF16)
    lora_b = lora.astype(BF16)
    for d, (lw_o, kd_o, b_o) in enumerate(((lw0_o, kd0_o, b0_o), (lw1_o, kd1_o, b1_o))):
        wl = -_softplus(-(w0_ref[d:d + 1, :] + _dot(lora_t, w2_ref[d]))) - 0.5
        lw_o[0] = -jnp.exp(wl)
        a = jax.nn.sigmoid(a0_ref[d:d + 1, :] + _dot(lora_b, a2_ref[d]))
        kd_o[0] = k * (1.0 + (a - 1.0) * ka_ref[...])
        b_o[0] = kk * a
    gate_o[0] = _dot(jax.nn.sigmoid(gate_in).astype(BF16), g2_ref[...])


def _rwkv_features(z, mu, w0, w2p, a0, a2p, g2p, k_k, k_a):
    b, t, zw = z.shape
    tt = min(t, 256)
    nh = t // HALO
    w = RWKV_WIDTH
    vec = pl.BlockSpec((1, w), lambda bi, i: (0, 0))
    out_spec = pl.BlockSpec((1, tt, w), lambda bi, i: (bi, i, 0))
    out = jax.ShapeDtypeStruct((b, t, w), F32)
    return pl.pallas_call(
        functools.partial(_rwkv_feat_kernel, tt=tt),
        grid=(b, t // tt),
        in_specs=[pl.BlockSpec((1, tt, zw), lambda bi, i: (bi, i, 0)),
                  pl.BlockSpec((1, HALO, zw), lambda bi, i: (bi, jnp.maximum(i * (tt // HALO) - 1, 0), 0)),
                  pl.BlockSpec((1, HALO, zw), lambda bi, i: (bi, jnp.minimum((i + 1) * (tt // HALO), nh - 1), 0)),
                  pl.BlockSpec((2, zw), lambda bi, i: (0, 0)),
                  pl.BlockSpec((2, w), lambda bi, i: (0, 0)),
                  pl.BlockSpec((2, LORA_IN, w), lambda bi, i: (0, 0, 0)),
                  pl.BlockSpec((2, w), lambda bi, i: (0, 0)),
                  pl.BlockSpec((2, LORA_IN, w), lambda bi, i: (0, 0, 0)),
                  pl.BlockSpec((GATE_PAD, w), lambda bi, i: (0, 0)),
                  vec, vec],
        out_specs=[out_spec] * 10,
        out_shape=[out] * 10,
        scratch_shapes=[pltpu.VMEM((tt + 2 * HALO, zw), F32)],
        compiler_params=_cparams("parallel", "parallel"),
        name="rwkv_features",
    )(z, z, z, mu, w0, w2p, a0, a2p, g2p, k_k.reshape(1, w), k_a.reshape(1, w))


def _rwkv_chunk_terms(chains, want_y):
    n = len(chains)
    rng = range(n)
    c = chains[0][1].shape[0]
    low_lane = chains[0][6][3]
    block_mask = chains[0][6][4]
    r, v, kk, lw, kd, beta = ([ch[k] for ch in chains] for k in range(6))
    tri_incl, incl2, strict2 = ([ch[6][k] for ch in chains] for k in range(3))

    def stack_heads(x):
        return jnp.concatenate([jnp.where(low_lane, x, 0.0), jnp.where(low_lane, 0.0, x)], axis=0)

    def unstack(x):
        return jnp.where(low_lane, x[:c], x[c:])

    cs = [_ones_dot(tri_incl[i], lw[i], 3) for i in rng]
    tot = [jnp.sum(lw[i], axis=0, keepdims=True) for i in rng]
    e_neg = [jnp.exp(-cs[i]) for i in rng]
    e_rem = [jnp.exp(tot[i] - cs[i]) for i in rng]
    kt_st = [stack_heads(kk[i] * jnp.exp(cs[i] - lw[i])) for i in rng]
    kt_sb = [kt_st[i].astype(BF16) for i in rng]
    bh_sb = [stack_heads(beta[i] * e_neg[i]).astype(BF16) for i in rng]
    kh_sb = [stack_heads(kd[i] * e_neg[i]).astype(BF16) for i in rng]
    right = [jnp.concatenate([beta[i] * e_rem[i], kd[i] * e_rem[i]], axis=0).astype(BF16) for i in rng]
    v_b = [v[i].astype(BF16) for i in rng]
    v2 = [jnp.concatenate([v_b[i], v_b[i]], axis=0) for i in rng]
    a_kb = [_dot_nt(kt_sb[i], bh_sb[i]) * strict2[i] for i in rng]
    a_kk = [(_dot_nt(kt_sb[i], kh_sb[i]) * strict2[i]).astype(BF16) for i in rng]
    if want_y:
        rt = [r[i] * jnp.exp(cs[i]) for i in rng]
        rt_sb = [stack_heads(rt[i]).astype(BF16) for i in rng]
        a_rb = [(_dot_nt(rt_sb[i], bh_sb[i]) * incl2[i]).astype(BF16) for i in rng]
        a_rk = [(_dot_nt(rt_sb[i], kh_sb[i]) * incl2[i]).astype(BF16) for i in rng]
    rhs0 = [_dot(a_kk[i], v2[i]) for i in rng]
    p = [-a_kb[i] for i in rng]
    x = list(p)
    for _ in range(int(np.log2(c)) - 1):
        p_b = [p[i].astype(BF16) for i in rng]
        x_b = [x[i].astype(BF16) for i in rng]
        p = [_dot(p_b[i], p_b[i]) for i in rng]
        xp = [_dot(x_b[i], p[i].astype(BF16)) for i in rng]
        x = [x[i] + p[i] + xp[i] for i in rng]
    both = [_dot(x[i].astype(BF16), jnp.concatenate([rhs0[i].astype(BF16), kt_sb[i]], axis=1)) for i in rng]
    u0 = [unstack(-(rhs0[i] + both[i][:, :LANES])) for i in rng]
    kq_st = [kt_st[i] + both[i][:, LANES:] for i in rng]
    kq = [unstack(kq_st[i]) for i in rng]
    lhs_t = [jnp.concatenate([jnp.concatenate([u0[i], v[i]], axis=0).T,
                              jnp.concatenate([kq[i], jnp.zeros_like(kq[i])], axis=0).T], axis=0).astype(BF16)
             for i in rng]
    gq = [_dot(lhs_t[i], right[i]) for i in rng]
    if want_y:
        u0_b = [u0[i].astype(BF16) for i in rng]
        y0 = [_dot(jnp.concatenate([a_rb[i], a_rk[i]], axis=1), jnp.concatenate([u0_b[i], u0_b[i], v2[i]], axis=0))
              for i in rng]
        rq = [_dot(a_rb[i], kq_st[i].astype(BF16)) for i in rng]
    out = []
    for i in rng:
        g_add = gq[i][:LANES] * block_mask
        q_mat = (gq[i][LANES:] * block_mask).astype(BF16)
        if want_y:
            out.append((jnp.exp(tot[i]), q_mat, g_add, (rt[i] - unstack(rq[i])).astype(BF16), unstack(y0[i])))
        else:
            out.append((jnp.exp(tot[i]), q_mat, g_add, None, None))
    return out


def _rwkv_consts(c, reverse):
    t = lax.broadcasted_iota(jnp.int32, (c, c), 0)
    s = lax.broadcasted_iota(jnp.int32, (c, c), 1)
    incl = (s >= t) if reverse else (s <= t)
    tri_incl = jnp.where(incl, 1.0, 0.0).astype(BF16)
    t2 = lax.broadcasted_iota(jnp.int32, (2 * c, 2 * c), 0)
    s2 = lax.broadcasted_iota(jnp.int32, (2 * c, 2 * c), 1)
    same = (t2 // c) == (s2 // c)
    order = (s2 >= t2) if reverse else (s2 <= t2)
    incl2 = jnp.where(same & order, 1.0, 0.0)
    strict2 = jnp.where(same & order & (s2 != t2), 1.0, 0.0)
    low_lane = lax.broadcasted_iota(jnp.int32, (1, LANES), 1) < RWKV_HEAD
    bi = lax.broadcasted_iota(jnp.int32, (LANES, LANES), 0) // RWKV_HEAD
    bj = lax.broadcasted_iota(jnp.int32, (LANES, LANES), 1) // RWKV_HEAD
    block_mask = jnp.where(bi == bj, 1.0, 0.0)
    return tri_incl, incl2, strict2, low_lane, block_mask


def _rwkv_kernel(r_l, v_l, kk_l, lw0_l, lw1_l, kd0_l, kd1_l, b0_l, b1_l, gate_l,
                 v_c, kk_c, lw0_c, lw1_c, kd0_c, kd1_c, b0_c, b1_c,
                 rk_ref, lng_ref, lnb_ref, o_ref, acc_s, gam_s, q_s, g_s, re_s):
    c = RWKV_CHUNK
    n_l, n_c = r_l.shape[1] // c, v_c.shape[1] // c
    consts = (_rwkv_consts(c, False), _rwkv_consts(c, True))
    lat_dirs = ((lw0_l, kd0_l, b0_l), (lw1_l, kd1_l, b1_l))
    ctx_dirs = ((lw0_c, kd0_c, b0_c), (lw1_c, kd1_c, b1_c))

    def terms_body(ii, carry, *, refs, dirs, slot0, want_y, group):
        r_ref, v_ref, kk_ref = refs
        chains, where = [], []
        for g in range(group):
            ci = ii * group + g
            rows = pl.ds(pl.multiple_of(ci * c, c), c)
            r = r_ref[0, rows, :] if want_y else None
            v, kk = v_ref[0, rows, :], kk_ref[0, rows, :]
            for d, (lw_r, kd_r, b_r) in enumerate(dirs):
                chains.append((r, v, kk, lw_r[0, rows, :], kd_r[0, rows, :], b_r[0, rows, :], consts[d]))
                where.append((d, ci, rows))
        terms = _rwkv_chunk_terms(chains, want_y)
        for (d, ci, rows), (gam, q_mat, g_add, r_eff, _) in zip(where, terms):
            gam_s[d, slot0 + ci] = gam
            q_s[d, slot0 + ci] = q_mat
            g_s[d, slot0 + ci] = g_add
            if want_y:
                re_s[d, ci] = r_eff
        if want_y:
            for g in range(group):
                acc_s[where[2 * g][2], :] = terms[2 * g][4] + terms[2 * g + 1][4]
        return carry

    g_c = min(RWKV_GROUP, n_c)
    lax.fori_loop(0, n_c // g_c, functools.partial(terms_body, refs=(None, v_c, kk_c), dirs=ctx_dirs, slot0=0,
                                                   want_y=False, group=g_c), 0)
    lax.fori_loop(0, n_l // RWKV_GROUP, functools.partial(terms_body, refs=(r_l, v_l, kk_l), dirs=lat_dirs,
                                                          slot0=n_c, want_y=True, group=RWKV_GROUP), 0)

    def advance(S, d, slot):
        s_b = S.astype(BF16)
        return S * gam_s[d, slot] - _dot(s_b, q_s[d, slot]) + g_s[d, slot], s_b

    def ctx_scan(k, carry):
        s_f, s_r = carry
        s_f, _ = advance(s_f, 0, k)
        s_r, _ = advance(s_r, 1, n_c - 1 - k)
        return s_f, s_r

    def lat_scan(k, carry):
        s_f, s_r = carry
        kr = n_l - 1 - k
        s_f, sf_b = advance(s_f, 0, n_c + k)
        s_r, sr_b = advance(s_r, 1, n_c + kr)
        rows_f = pl.ds(pl.multiple_of(k * c, c), c)
        rows_r = pl.ds(pl.multiple_of(kr * c, c), c)
        acc_s[rows_f, :] += _dot_nt(re_s[0, k], sf_b)
        acc_s[rows_r, :] += _dot_nt(re_s[1, kr], sr_b)
        return s_f, s_r

    zero = jnp.zeros((LANES, LANES), F32)
    carry = lax.fori_loop(0, n_c, ctx_scan, (zero, zero))
    lax.fori_loop(0, n_l, lat_scan, carry)

    ones = _head_ones(LANES, RWKV_HEAD)
    wkv = acc_s[...]
    inv_n = 1.0 / RWKV_HEAD
    mean = _split_dot(wkv, ones, 2) * inv_n
    xc = wkv - mean
    var = _split_dot(xc * xc, ones, 2) * inv_n
    y = xc * lax.rsqrt(var + RWKV_GN_EPS) * lng_ref[...] + lnb_ref[...]
    r = r_l[0]
    rk = rk_ref[...]
    bonus = (_split_dot(r * kd0_l[0] * rk, ones, 2) + _split_dot(r * kd1_l[0] * rk, ones, 2)) * v_l[0]
    o_ref[0] = ((y + bonus) * gate_l[0]).astype(o_ref.dtype)


def _rwkv_mix(feat_l, feat_c, r_k, ln_g, ln_b):
    r_l, v_l, kk_l, lw0_l, lw1_l, kd0_l, kd1_l, b0_l, b1_l, gate_l = feat_l
    _, v_c, kk_c, lw0_c, lw1_c, kd0_c, kd1_c, b0_c, b1_c, _ = feat_c
    b, t, w = r_l.shape
    t_c = v_c.shape[1]
    n_slots = (t_c + t) // RWKV_CHUNK
    lat = pl.BlockSpec((1, t, LANES), lambda bi, p: (bi, 0, p))
    ctx = pl.BlockSpec((1, t_c, LANES), lambda bi, p: (bi, 0, p))
    vec = pl.BlockSpec((1, LANES), lambda bi, p: (0, p))
    return pl.pallas_call(
        _rwkv_kernel,
        grid=(b, w // LANES),
        in_specs=[lat] * 10 + [ctx] * 8 + [vec] * 3,
        out_specs=lat,
        out_shape=jax.ShapeDtypeStruct((b, t, w), BF16),
        scratch_shapes=[pltpu.VMEM((t, LANES), F32),
                        pltpu.VMEM((2, n_slots, 1, LANES), F32),
                        pltpu.VMEM((2, n_slots, LANES, LANES), BF16),
                        pltpu.VMEM((2, n_slots, LANES, LANES), F32),
                        pltpu.VMEM((2, t // RWKV_CHUNK, RWKV_CHUNK, LANES), BF16)],
        compiler_params=_cparams("parallel", "parallel"),
        name="rwkv_mix",
    )(r_l, v_l, kk_l, lw0_l, lw1_l, kd0_l, kd1_l, b0_l, b1_l, gate_l,
      v_c, kk_c, lw0_c, lw1_c, kd0_c, kd1_c, b0_c, b1_c,
      r_k.reshape(1, w), ln_g.reshape(1, w), ln_b.reshape(1, w))


def _retention_kernel(dl_ref, q_l, k_l, v_l, rg_l, k_c, v_c, cos_ref, sin_ref, gg_ref, gb_ref, o_ref,
                      q_s, k_s, kv_s, rin_s):
    c = RET_CHUNK
    n_l, n_c = q_l.shape[1] // c, k_c.shape[1] // c
    h = pl.program_id(1)
    k_scale = HEAD_DIM ** -0.5
    q_s[...] = _rope(q_l[0].astype(F32), cos_ref[...], sin_ref[...]).astype(BF16)
    k_s[...] = (_rope(k_l[0].astype(F32), cos_ref[...], sin_ref[...]) * k_scale).astype(BF16)
    ti = lax.broadcasted_iota(jnp.int32, (c, c), 0).astype(F32)
    si = lax.broadcasted_iota(jnp.int32, (c, c), 1).astype(F32)
    col = lax.broadcasted_iota(jnp.int32, (c, 1), 0).astype(F32)
    row = lax.broadcasted_iota(jnp.int32, (1, c), 1).astype(F32)
    lg = [jax.nn.log_sigmoid(jnp.full((1, LANES), dl_ref[d, h], F32))[:, :1] for d in range(2)]
    fwd = jnp.where(ti >= si, jnp.exp(lg[0] * jnp.maximum(ti - si, 0.0)), 0.0)
    bwd = jnp.where(si >= ti, jnp.exp(lg[1] * jnp.maximum(si - ti, 0.0)), 0.0)
    inner = fwd + bwd
    q_decay = (jnp.exp(lg[0] * (col + 1.0)), jnp.exp(lg[1] * (c - col)))
    k_decay = (jnp.exp(lg[0] * (c - 1.0 - row)), jnp.exp(lg[1] * row))
    chunk_decay = (jnp.exp(lg[0] * c), jnp.exp(lg[1] * c))

    def kv_body(ii, carry, *, group, slot0, lat):
        ks, vs, slots = [], [], []
        for g in range(group):
            ci = ii * group + g
            rows = pl.ds(pl.multiple_of(ci * c, c), c)
            ks.append(k_s[rows, :].astype(F32) if lat else k_c[0, rows, :].astype(F32) * k_scale)
            vs.append(v_l[0, rows, :] if lat else v_c[0, rows, :])
            slots.append(slot0 + ci)
        kts = [k.T for k in ks]
        lhs = [jnp.concatenate([kt * k_decay[0], kt * k_decay[1]], axis=0).astype(BF16) for kt in kts]
        kv = [_dot(a, v) for a, v in zip(lhs, vs)]
        for slot, x in zip(slots, kv):
            kv_s[0, slot] = x[:HEAD_DIM]
            kv_s[1, slot] = x[HEAD_DIM:]
        return carry

    g_c = min(RET_GROUP, n_c)
    lax.fori_loop(0, n_c // g_c, functools.partial(kv_body, group=g_c, slot0=0, lat=False), 0)
    lax.fori_loop(0, n_l // RET_GROUP, functools.partial(kv_body, group=RET_GROUP, slot0=n_c, lat=True), 0)

    def scan_body(k, carry, *, n, slot0, record):
        r_f, r_b = carry
        kb = n - 1 - k
        if record:
            rin_s[0, k] = r_f.astype(BF16)
            rin_s[1, kb] = r_b.astype(BF16)
        return r_f * chunk_decay[0] + kv_s[0, slot0 + k], r_b * chunk_decay[1] + kv_s[1, slot0 + kb]

    zero = jnp.zeros((HEAD_DIM, RET_V_DIM), F32)
    carry = lax.fori_loop(0, n_c, functools.partial(scan_body, n=n_c, slot0=0, record=False), (zero, zero))
    lax.fori_loop(0, n_l, functools.partial(scan_body, n=n_l, slot0=n_c, record=True), carry)

    def out_body(ii, carry):
        qs, kbs, vs, rows_l = [], [], [], []
        for g in range(RET_GROUP):
            ci = ii * RET_GROUP + g
            rows = pl.ds(pl.multiple_of(ci * c, c), c)
            rows_l.append((ci, rows))
            qs.append(q_s[rows, :])
            kbs.append(k_s[rows, :])
            vs.append(v_l[0, rows, :])
        s = [(_dot_nt(q, k) * inner).astype(BF16) for q, k in zip(qs, kbs)]
        qd = []
        for q in qs:
            qf = q.astype(F32)
            qd.append(jnp.concatenate([qf * q_decay[0], qf * q_decay[1]], axis=1).astype(BF16))
        r_in = [jnp.concatenate([rin_s[0, ci], rin_s[1, ci]], axis=0) for ci, _ in rows_l]
        o = [_dot(s[g], vs[g]) + _dot(qd[g], r_in[g]) for g in range(RET_GROUP)]
        for (ci, rows), x in zip(rows_l, o):
            xc = x - jnp.mean(x, axis=-1, keepdims=True)
            var = jnp.mean(xc * xc, axis=-1, keepdims=True)
            y = xc * lax.rsqrt(var + NORM_EPS) * gg_ref[...] + gb_ref[...]
            o_ref[0, rows, :] = (y * _silu(rg_l[0, rows, :].astype(F32))).astype(o_ref.dtype)
        return carry

    lax.fori_loop(0, n_l // RET_GROUP, out_body, 0)


def _retention(p_l, p_c, decay_logit, gn_g, gn_b, cos_t, sin_t):
    b, t, _ = p_l.shape
    t_c = p_c.shape[1]
    nh, dk, dv = RET_HEADS, HEAD_DIM, RET_V_DIM
    k_blk, v_blk, g_blk = nh, (2 * nh * dk) // dv, (2 * nh * dk + nh * dv) // dv
    return pl.pallas_call(
        _retention_kernel,
        grid=(b, nh),
        in_specs=[pl.BlockSpec(memory_space=pltpu.SMEM),
                  pl.BlockSpec((1, t, dk), lambda bi, h: (bi, 0, h)),
                  pl.BlockSpec((1, t, dk), lambda bi, h: (bi, 0, k_blk + h)),
                  pl.BlockSpec((1, t, dv), lambda bi, h: (bi, 0, v_blk + h)),
                  pl.BlockSpec((1, t, dv), lambda bi, h: (bi, 0, g_blk + h)),
                  pl.BlockSpec((1, t_c, dk), lambda bi, h: (bi, 0, k_blk + h)),
                  pl.BlockSpec((1, t_c, dv), lambda bi, h: (bi, 0, v_blk + h)),
                  pl.BlockSpec((t, dk), lambda bi, h: (0, 0)),
                  pl.BlockSpec((t, dk), lambda bi, h: (0, 0)),
                  pl.BlockSpec((1, dv), lambda bi, h: (0, h)),
                  pl.BlockSpec((1, dv), lambda bi, h: (0, h))],
        out_specs=pl.BlockSpec((1, t, dv), lambda bi, h: (bi, 0, h)),
        out_shape=jax.ShapeDtypeStruct((b, t, nh * dv), BF16),
        scratch_shapes=[pltpu.VMEM((t, dk), BF16), pltpu.VMEM((t, dk), BF16),
                        pltpu.VMEM((2, (t_c + t) // RET_CHUNK, dk, dv), F32),
                        pltpu.VMEM((2, t // RET_CHUNK, dk, dv), BF16)],
        compiler_params=_cparams("parallel", "parallel"),
        name="retention",
    )(decay_logit, p_l, p_l, p_l, p_l, p_c, p_c, cos_t, sin_t, gn_g.reshape(1, nh * dv), gn_b.reshape(1, nh * dv))


def kernel(x, c, ctx, c_ctx, mod_w, mod_b, norm_mix_g, norm_ffn_g, ffn_w_up, ffn_conv_w, ffn_conv_b, ffn_w_down, ab_w_in, ab_q_norm, ab_k_norm, ab_conv_w, ab_conv_b, ab_conv_norm_g, ab_conv_norm_b, ab_w_out, cd_w_in, cd_shift_mu, rwkv_w0, rwkv_w2, rwkv_a0, rwkv_a2, rwkv_g2, rwkv_k_k, rwkv_k_a, rwkv_r_k, rwkv_ln_g, rwkv_ln_b, ret_decay_logit, ret_gn_g, ret_gn_b, cd_w_out, final_norm_g):
    b, t, d = x.shape
    t_c = ctx.shape[1]
    att_w = ATT_HEADS * HEAD_DIM
    kv_w = att_w // ATT_GROUP

    pad_rows = (-(b + 1)) % 8
    c_all = jnp.concatenate([c, c_ctx[None, :], jnp.zeros((pad_rows, d), F32)], axis=0)
    mod = _modulation(c_all, mod_w, mod_b)

    def mod_vectors(layer):
        lat = [m.reshape(b, 1, d) for m in jnp.split(mod[layer, :b], 6, axis=-1)]
        cx = [m.reshape(1, 1, d) for m in jnp.split(mod[layer, b:b + 1], 6, axis=-1)]
        return lat, cx

    cos_t, sin_t = _rope_tables(t)
    h = x
    hc = ctx.reshape(1, b * t_c, d)

    (sh1, sc1, g1, sh2, sc2, g2), (csh1, csc1, cg1, csh2, csc2, cg2) = mod_vectors(0)
    w_in = ab_w_in[0]
    glu0 = att_w + 2 * kv_w
    w_in = jnp.concatenate([w_in[:, glu0:], w_in[:, :glu0]], axis=1).astype(BF16)
    p_l = _norm_mod_matmul(h, norm_mix_g[0], sh1, sc1, w_in, BF16, "ab_in_lat")
    p_c = _norm_mod_matmul(hc, norm_mix_g[0], csh1, csc1, w_in, BF16, "ab_in_ctx").reshape(b, t_c, -1)
    att_l = _attention(p_l, p_c, p_l, ab_q_norm[0], ab_k_norm[0], cos_t, sin_t)
    att_c = _attention(p_c, p_c, None, ab_q_norm[0], ab_k_norm[0], None, None)
    cv_l = _conformer(p_l, ab_conv_w[0], ab_conv_b[0], ab_conv_norm_g[0], ab_conv_norm_b[0])
    cv_c = _conformer(p_c, ab_conv_w[0], ab_conv_b[0], ab_conv_norm_g[0], ab_conv_norm_b[0])
    w_out = ab_w_out[0].astype(BF16)
    w_o = [w_out[:att_w], w_out[att_w:]]
    h = _matmul_gated_residual([att_l, cv_l], w_o, h, g1, "ab_out_lat")
    hc = _matmul_gated_residual([att_c.reshape(1, b * t_c, -1), cv_c.reshape(1, b * t_c, -1)], w_o, hc, cg1,
                                "ab_out_ctx")
    w_up, w_down = ffn_w_up[0].astype(BF16), ffn_w_down[0].astype(BF16)
    h = _conv_ffn(h, norm_ffn_g[0], sh2, sc2, g2, w_up, ffn_conv_w[0], ffn_conv_b[0], w_down, final_norm_g,
                  t, False, "ffn0_lat")
    hc = _conv_ffn(hc, norm_ffn_g[0], csh2, csc2, cg2, w_up, ffn_conv_w[0], ffn_conv_b[0], w_down,
                   final_norm_g, t_c, False, "ffn0_ctx")

    (sh1, sc1, g1, sh2, sc2, g2), (csh1, csc1, _, _, _, _) = mod_vectors(1)
    rw_in = 3 * RWKV_WIDTH + 2 * RWKV_HEAD + GATE_LORA
    zpad = RWKV_IN_PAD - rw_in
    w_z = jnp.pad(cd_w_in[0][:, :rw_in], ((0, 0), (0, zpad))).astype(BF16)
    w_ret = cd_w_in[0][:, rw_in:].astype(BF16)
    z_l = _norm_mod_matmul(h, norm_mix_g[1], sh1, sc1, w_z, F32, "cd_in_z_lat")
    z_c = _norm_mod_matmul(hc, norm_mix_g[1], csh1, csc1, w_z, F32, "cd_in_z_ctx").reshape(b, t_c, -1)
    rp_l = _norm_mod_matmul(h, norm_mix_g[1], sh1, sc1, w_ret, BF16, "cd_in_ret_lat")
    rp_c = _norm_mod_matmul(hc, norm_mix_g[1], csh1, csc1, w_ret, BF16, "cd_in_ret_ctx").reshape(b, t_c, -1)

    mu = jnp.pad(cd_shift_mu[0], ((0, 0), (0, zpad)))
    lora_pad = LORA_IN - RWKV_HEAD
    w2p = jnp.pad(rwkv_w2[0], ((0, 0), (0, lora_pad), (0, 0))).astype(BF16)
    a2p = jnp.pad(rwkv_a2[0], ((0, 0), (lora_pad, 0), (0, 0))).astype(BF16)
    g2p = jnp.pad(rwkv_g2[0], ((0, GATE_PAD - GATE_LORA), (0, 0))).astype(BF16)
    feat_l = _rwkv_features(z_l, mu, rwkv_w0[0], w2p, rwkv_a0[0], a2p, g2p, rwkv_k_k[0], rwkv_k_a[0])
    feat_c = _rwkv_features(z_c, mu, rwkv_w0[0], w2p, rwkv_a0[0], a2p, g2p, rwkv_k_k[0], rwkv_k_a[0])
    y_c = _rwkv_mix(feat_l, feat_c, rwkv_r_k[0], rwkv_ln_g[0], rwkv_ln_b[0])
    y_d = _retention(rp_l, rp_c, ret_decay_logit[0], ret_gn_g[0], ret_gn_b[0], cos_t, sin_t)
    w_out = cd_w_out[0].astype(BF16)
    h = _matmul_gated_residual([y_c, y_d], [w_out[:RWKV_WIDTH], w_out[RWKV_WIDTH:]], h, g1, "cd_out_lat")
    w_up, w_down = ffn_w_up[1].astype(BF16), ffn_w_down[1].astype(BF16)
    return _conv_ffn(h, norm_ffn_g[1], sh2, sc2, g2, w_up, ffn_conv_w[1], ffn_conv_b[1], w_down, final_norm_g,
                     t, True, "ffn1_lat")
```

```python
import functools

import jax
import jax.numpy as jnp
import numpy as np
from jax import lax
from jax.experimental import pallas as pl
from jax.experimental.pallas import tpu as pltpu

F32 = jnp.float32
BF16 = jnp.bfloat16

NORM_EPS = 1e-6
GRID_W = 64
ROPE_BASE = 10000.0
HEAD_DIM = 128
ATT_HEADS = 8
ATT_GROUP = 4
CONV_WIDTH = 1024
CONV_KERNEL = 31
RWKV_WIDTH = 1024
RWKV_HEAD = 64
RWKV_GN_EPS = 64e-5
RWKV_CHUNK = 64
RWKV_GROUP = 8
LORA_IN = 128
GATE_LORA = 160
GATE_PAD = 384
RWKV_IN_PAD = 3584
RET_HEADS = 8
RET_V_DIM = 256
RET_CHUNK = 128
RET_GROUP = 4
FFN_DIM = 5632
HALO = 16
NORM_PIECE = 16
FFN_ROW_TILE = 1024
FFN_ACC_COLS = 512
FFN_OUT_ROWS = 64
INPROJ_ROW_CHUNK = 256
LANES = 128
SUBLANES = 8
LOG2_E = 1.4426950408889634

VMEM_LIMIT_BYTES = 52 * 1024 * 1024


def _cparams(*sem):
    return pltpu.CompilerParams(dimension_semantics=sem, vmem_limit_bytes=VMEM_LIMIT_BYTES)


def _dot(a, b):
    return jnp.dot(a, b, preferred_element_type=F32)


def _dot_nt(a, b):
    return lax.dot_general(a, b, (((1,), (1,)), ((), ())), preferred_element_type=F32)


def _split_dot(x, ones_bf16, terms):
    acc = None
    rem = x
    for _ in range(terms):
        piece = rem.astype(BF16)
        part = _dot(piece, ones_bf16)
        acc = part if acc is None else acc + part
        rem = rem - piece.astype(F32)
    return acc


def _ones_dot(ones_bf16, x, terms):
    acc = None
    rem = x
    for _ in range(terms):
        piece = rem.astype(BF16)
        part = _dot(ones_bf16, piece)
        acc = part if acc is None else acc + part
        rem = rem - piece.astype(F32)
    return acc


def _rms(x, g):
    return x * lax.rsqrt(jnp.mean(x * x, axis=-1, keepdims=True) + NORM_EPS) * g


def _silu(x):
    return x * jax.nn.sigmoid(x)


def _rope(x, cos, sin_signed):
    n = x.shape[-1]
    lane = lax.broadcasted_iota(jnp.int32, x.shape, x.ndim - 1)
    partner = jnp.where((lane % 64) < 32, pltpu.roll(x, n - 32, x.ndim - 1), pltpu.roll(x, 32, x.ndim - 1))
    return x * cos + partner * sin_signed


def _rope_tables(n_tok):
    rows = n_tok // GRID_W
    row = jnp.repeat(jnp.arange(rows, dtype=F32), GRID_W)
    col = jnp.tile(jnp.arange(GRID_W, dtype=F32), rows)
    axis_dim = HEAD_DIM // 2
    inv_freq = ROPE_BASE ** (-jnp.arange(0, axis_dim, 2, dtype=F32) / axis_dim)
    ang = jnp.concatenate([row[:, None] * inv_freq, col[:, None] * inv_freq], axis=-1)
    cos, sin = jnp.cos(ang), jnp.sin(ang)
    c_r, c_c, s_r, s_c = cos[:, :32], cos[:, 32:], sin[:, :32], sin[:, 32:]
    cos_t = jnp.concatenate([c_r, c_r, c_c, c_c], axis=-1)
    sin_t = jnp.concatenate([-s_r, s_r, -s_c, s_c], axis=-1)
    return cos_t, sin_t


def _mod_kernel(c_ref, w_ref, b_ref, o_ref):
    s = _silu(c_ref[...]).astype(BF16)
    o_ref[0] = _dot(s, w_ref[0].astype(BF16)) + b_ref[0]


def _modulation(c_all, mod_w, mod_b):
    n_layers, d, n = mod_w.shape
    rows = c_all.shape[0]
    tn = 1024
    return pl.pallas_call(
        _mod_kernel,
        grid=(n_layers, n // tn),
        in_specs=[pl.BlockSpec((rows, d), lambda l, j: (0, 0)),
                  pl.BlockSpec((1, d, tn), lambda l, j: (l, 0, j)),
                  pl.BlockSpec((1, 1, tn), lambda l, j: (l, 0, j))],
        out_specs=pl.BlockSpec((1, rows, tn), lambda l, j: (l, 0, j)),
        out_shape=jax.ShapeDtypeStruct((n_layers, rows, n), F32),
        compiler_params=_cparams("parallel", "parallel"),
        name="modulation",
    )(c_all, mod_w, mod_b.reshape(n_layers, 1, n))


def _norm_mod_rows(src, dst, dst_row0, n_rows, gain, shift):
    for r0 in range(0, n_rows, NORM_PIECE):
        x = src(r0)
        inv = lax.rsqrt(jnp.mean(x * x, axis=-1, keepdims=True) + NORM_EPS)
        dst[dst_row0 + r0:dst_row0 + r0 + NORM_PIECE, :] = (x * inv * gain + shift).astype(dst.dtype)


def _inproj_kernel(h_ref, g_ref, sh_ref, sc_ref, w_ref, o_ref, u_s, *, tm):
    j = pl.program_id(2)

    @pl.when(j == 0)
    def _():
        gain = g_ref[...] * (1.0 + sc_ref[0])
        shift = sh_ref[0]
        chunk = min(tm, INPROJ_ROW_CHUNK)
        for c0 in range(0, tm, chunk):
            _norm_mod_rows(lambda r0: h_ref[0, c0 + r0:c0 + r0 + NORM_PIECE, :], u_s, c0, chunk, gain, shift)
            o_ref[0, c0:c0 + chunk, :] = _dot(u_s[c0:c0 + chunk, :], w_ref[...]).astype(o_ref.dtype)

    @pl.when(j > 0)
    def _():
        o_ref[0] = _dot(u_s[...], w_ref[...]).astype(o_ref.dtype)


def _norm_mod_matmul(h, g, shift, scale, w, out_dtype, name):
    bm, t, d = h.shape
    n = w.shape[1]
    tm = min(t, 1024)
    tn = 512
    return pl.pallas_call(
        functools.partial(_inproj_kernel, tm=tm),
        grid=(bm, t // tm, n // tn),
        in_specs=[pl.BlockSpec((1, tm, d), lambda b, i, j: (b, i, 0)),
                  pl.BlockSpec((1, d), lambda b, i, j: (0, 0)),
                  pl.BlockSpec((1, 1, d), lambda b, i, j: (b, 0, 0)),
                  pl.BlockSpec((1, 1, d), lambda b, i, j: (b, 0, 0)),
                  pl.BlockSpec((d, tn), lambda b, i, j: (0, j))],
        out_specs=pl.BlockSpec((1, tm, tn), lambda b, i, j: (b, i, j)),
        out_shape=jax.ShapeDtypeStruct((bm, t, n), out_dtype),
        scratch_shapes=[pltpu.VMEM((tm, d), BF16)],
        compiler_params=_cparams("parallel", "parallel", "arbitrary"),
        name=name,
    )(h, g.reshape(1, d), shift, scale, w)


def _outproj_kernel(*refs, n_lhs):
    lhs, ws = refs[:n_lhs], refs[n_lhs:2 * n_lhs]
    res_ref, gt_ref, o_ref = refs[2 * n_lhs:]
    acc = _dot(lhs[0][0], ws[0][...])
    for a, w in zip(lhs[1:], ws[1:]):
        acc = acc + _dot(a[0], w[...])
    o_ref[0] = res_ref[0] + gt_ref[0] * acc


def _matmul_gated_residual(lhs_list, w_list, res, gate, name):
    bm, t, d = res.shape
    tm = min(t, 1024)
    tn = 512
    in_specs = [pl.BlockSpec((1, tm, a.shape[-1]), lambda b, i, j: (b, i, 0)) for a in lhs_list]
    in_specs += [pl.BlockSpec((w.shape[0], tn), lambda b, i, j: (0, j)) for w in w_list]
    in_specs += [pl.BlockSpec((1, tm, tn), lambda b, i, j: (b, i, j)),
                 pl.BlockSpec((1, 1, tn), lambda b, i, j: (b, 0, j))]
    return pl.pallas_call(
        functools.partial(_outproj_kernel, n_lhs=len(lhs_list)),
        grid=(bm, t // tm, d // tn),
        in_specs=in_specs,
        out_specs=pl.BlockSpec((1, tm, tn), lambda b, i, j: (b, i, j)),
        out_shape=jax.ShapeDtypeStruct((bm, t, d), F32),
        compiler_params=_cparams("parallel", "parallel", "arbitrary"),
        name=name,
    )(*lhs_list, *w_list, res, gate)


def _ffn_kernel(h_ref, hp_ref, hn_ref, g_ref, sh_ref, sc_ref, gt_ref, wg_ref, wv_ref, cw_ref, cb_ref,
                wd_ref, fg_ref, o_ref, u_s, *, tm, seq_len, final_norm):
    i, j, nj = pl.program_id(1), pl.program_id(2), pl.num_programs(2)

    @pl.when(j == 0)
    def _():
        gain = g_ref[...] * (1.0 + sc_ref[0])
        shift = sh_ref[0]
        _norm_mod_rows(lambda r0: hp_ref[0, r0:r0 + NORM_PIECE, :], u_s, 0, HALO, gain, shift)
        _norm_mod_rows(lambda r0: h_ref[0, r0:r0 + NORM_PIECE, :], u_s, HALO, tm, gain, shift)
        _norm_mod_rows(lambda r0: hn_ref[0, r0:r0 + NORM_PIECE, :], u_s, HALO + tm, HALO, gain, shift)
        o_ref[0] = jnp.zeros(o_ref.shape[1:], F32)

    rows = tm + 2 * HALO
    gate_lin = _dot(u_s[...], wg_ref[...])
    val = _dot(u_s[HALO:HALO + tm], wv_ref[...])
    pos = (i * tm + lax.broadcasted_iota(jnp.int32, (tm, 1), 0)) % seq_len
    g_prev = jnp.where(pos != 0, pltpu.roll(gate_lin, 1, 0)[HALO:HALO + tm], 0.0)
    g_next = jnp.where(pos != seq_len - 1, pltpu.roll(gate_lin, rows - 1, 0)[HALO:HALO + tm], 0.0)
    g_cur = gate_lin[HALO:HALO + tm]
    cw = cw_ref[...]
    conv = cw[0:1] * g_prev + cw[1:2] * g_cur + cw[2:3] * g_next + cb_ref[...]
    act = (_silu(conv) * val).astype(BF16)
    d = o_ref.shape[2]
    for c0 in range(0, d, FFN_ACC_COLS):
        o_ref[0, :, c0:c0 + FFN_ACC_COLS] += _dot(act, wd_ref[:, c0:c0 + FFN_ACC_COLS])

    @pl.when(j == nj - 1)
    def _():
        for r0 in range(0, tm, FFN_OUT_ROWS):
            rs = slice(r0, r0 + FFN_OUT_ROWS)
            y = h_ref[0, rs, :] + gt_ref[0] * o_ref[0, rs, :]
            if final_norm:
                y = _rms(y, fg_ref[...])
            o_ref[0, rs, :] = y


def _conv_ffn(h, g, shift, scale, gate, w_up, conv_w, conv_b, w_down, layer, final_g, seq_len, final_norm, name):
    bm, t, d = h.shape
    f = w_down.shape[1]
    tm = min(t, FFN_ROW_TILE)
    tf = 512
    nh = t // HALO
    kern = functools.partial(_ffn_kernel, tm=tm, seq_len=seq_len, final_norm=final_norm)
    vec = pl.BlockSpec((1, 1, d), lambda b, i, j: (b, 0, 0))
    return pl.pallas_call(
        kern,
        grid=(bm, t // tm, f // tf),
        in_specs=[pl.BlockSpec((1, tm, d), lambda b, i, j: (b, i, 0), pipeline_mode=pl.Buffered(1)),
                  pl.BlockSpec((1, HALO, d), lambda b, i, j: (b, jnp.maximum(i * (tm // HALO) - 1, 0), 0)),
                  pl.BlockSpec((1, HALO, d), lambda b, i, j: (b, jnp.minimum((i + 1) * (tm // HALO), nh - 1), 0)),
                  pl.BlockSpec((1, d), lambda b, i, j: (0, 0)),
                  vec, vec, vec,
                  pl.BlockSpec((None, d, tf), lambda b, i, j: (layer, 0, j)),
                  pl.BlockSpec((None, d, tf), lambda b, i, j: (layer, 0, f // tf + j)),
                  pl.BlockSpec((3, tf), lambda b, i, j: (0, j)),
                  pl.BlockSpec((1, tf), lambda b, i, j: (0, j)),
                  pl.BlockSpec((None, tf, d), lambda b, i, j: (layer, j, 0)),
                  pl.BlockSpec((1, d), lambda b, i, j: (0, 0))],
        out_specs=pl.BlockSpec((1, tm, d), lambda b, i, j: (b, i, 0)),
        out_shape=jax.ShapeDtypeStruct((bm, t, d), F32),
        scratch_shapes=[pltpu.VMEM((tm + 2 * HALO, d), BF16)],
        compiler_params=_cparams("parallel", "parallel", "arbitrary"),
        name=name,
    )(h, h, h, g.reshape(1, d), shift, scale, gate, w_up, w_up, conv_w, conv_b.reshape(1, f), w_down,
      final_g.reshape(1, d))


def _attn_kernel(*refs, has_lat):
    if has_lat:
        (q_ref, kc_ref, vc_ref, kl_ref, vl_ref, qn_ref, kn_ref, cq_ref, sq_ref, ck_ref, sk_ref,
         o_ref, kc_s, kl_s) = refs
    else:
        q_ref, kc_ref, vc_ref, qn_ref, kn_ref, o_ref, kc_s = refs

    @pl.when(pl.program_id(2) == 0)
    def _():
        kc_s[...] = _rms(kc_ref[0].astype(F32), kn_ref[...]).astype(BF16)
        if has_lat:
            kl = _rms(kl_ref[0].astype(F32), kn_ref[...])
            kl_s[...] = _rope(kl, ck_ref[...], sk_ref[...]).astype(BF16)

    scale = HEAD_DIM ** -0.5 * LOG2_E
    for hh in range(ATT_GROUP):
        cols = slice(hh * HEAD_DIM, (hh + 1) * HEAD_DIM)
        q = _rms(q_ref[0, :, cols].astype(F32), qn_ref[...])
        if has_lat:
            q = _rope(q, cq_ref[...], sq_ref[...])
        q = (q * scale).astype(BF16)
        s_c = _dot_nt(q, kc_s[...])
        m = jnp.max(s_c, axis=-1, keepdims=True)
        if has_lat:
            s_l = _dot_nt(q, kl_s[...])
            m = jnp.maximum(m, jnp.max(s_l, axis=-1, keepdims=True))
        p_c = jnp.exp2(s_c - m)
        denom = jnp.sum(p_c, axis=-1, keepdims=True)
        acc = _dot(p_c.astype(BF16), vc_ref[0])
        if has_lat:
            p_l = jnp.exp2(s_l - m)
            denom = denom + jnp.sum(p_l, axis=-1, keepdims=True)
            acc = acc + _dot(p_l.astype(BF16), vl_ref[0])
        o_ref[0, :, cols] = (acc / denom).astype(o_ref.dtype)


def _attention(p_q, p_ctx, p_lat, q_norm, k_norm, cos_t, sin_t):
    b, t, _ = p_q.shape
    n_c = p_ctx.shape[1]
    has_lat = p_lat is not None
    tq = min(t, 256)
    gw = ATT_GROUP * HEAD_DIM
    q_blk, k_blk, v_blk = 2048 // gw, (2048 + 1024) // HEAD_DIM, (2048 + 1024 + 256) // HEAD_DIM
    vec = pl.BlockSpec((1, HEAD_DIM), lambda bi, g, qi: (0, 0))
    in_specs = [pl.BlockSpec((1, tq, gw), lambda bi, g, qi: (bi, qi, q_blk + g)),
                pl.BlockSpec((1, n_c, HEAD_DIM), lambda bi, g, qi: (bi, 0, k_blk + g)),
                pl.BlockSpec((1, n_c, HEAD_DIM), lambda bi, g, qi: (bi, 0, v_blk + g))]
    args = [p_q, p_ctx, p_ctx]
    scratch = [pltpu.VMEM((n_c, HEAD_DIM), BF16)]
    if has_lat:
        n_l = p_lat.shape[1]
        in_specs += [pl.BlockSpec((1, n_l, HEAD_DIM), lambda bi, g, qi: (bi, 0, k_blk + g)),
                     pl.BlockSpec((1, n_l, HEAD_DIM), lambda bi, g, qi: (bi, 0, v_blk + g))]
        args += [p_lat, p_lat]
        scratch.append(pltpu.VMEM((n_l, HEAD_DIM), BF16))
    in_specs += [vec, vec]
    args += [q_norm.reshape(1, HEAD_DIM), k_norm.reshape(1, HEAD_DIM)]
    if has_lat:
        in_specs += [pl.BlockSpec((tq, HEAD_DIM), lambda bi, g, qi: (qi, 0)),
                     pl.BlockSpec((tq, HEAD_DIM), lambda bi, g, qi: (qi, 0)),
                     pl.BlockSpec((n_l, HEAD_DIM), lambda bi, g, qi: (0, 0)),
                     pl.BlockSpec((n_l, HEAD_DIM), lambda bi, g, qi: (0, 0))]
        args += [cos_t, sin_t, cos_t, sin_t]
    return pl.pallas_call(
        functools.partial(_attn_kernel, has_lat=has_lat),
        grid=(b, ATT_HEADS // ATT_GROUP, t // tq),
        in_specs=in_specs,
        out_specs=pl.BlockSpec((1, tq, gw), lambda bi, g, qi: (bi, qi, g)),
        out_shape=jax.ShapeDtypeStruct((b, t, ATT_HEADS * HEAD_DIM), BF16),
        scratch_shapes=scratch,
        compiler_params=_cparams("parallel", "parallel", "arbitrary"),
        name="attention_lat" if has_lat else "attention_ctx",
    )(*args)


def _conformer_kernel(x_ref, xp_ref, xn_ref, w_ref, b_ref, g_ref, be_ref, o_ref, xs, *, tt):
    i, n = pl.program_id(1), pl.num_programs(1)

    def glu(ref):
        v = ref[0].astype(F32)
        return v[:, :CONV_WIDTH] * jax.nn.sigmoid(v[:, CONV_WIDTH:])

    xs[0:HALO] = jnp.where(i > 0, glu(xp_ref), 0.0)
    xs[HALO:HALO + tt] = glu(x_ref)
    xs[HALO + tt:] = jnp.where(i < n - 1, glu(xn_ref), 0.0)
    pad = CONV_KERNEL // 2
    offs = [HALO - pad + k for k in range(CONV_KERNEL)]
    window = xs[...]
    rows = tt + 2 * HALO
    acc = None
    for phase in range(SUBLANES):
        group = [o for o in offs if o % SUBLANES == phase]
        if not group:
            continue
        base = window if phase == 0 else pltpu.roll(window, rows - phase, 0)
        for o in group:
            k = o - (HALO - pad)
            term = w_ref[k:k + 1, :] * base[o - phase:o - phase + tt, :]
            acc = term if acc is None else acc + term
    h = acc + b_ref[...]
    hc = h - jnp.mean(h, axis=-1, keepdims=True)
    var = jnp.mean(hc * hc, axis=-1, keepdims=True)
    y = hc * lax.rsqrt(var + NORM_EPS) * g_ref[...] + be_ref[...]
    o_ref[0] = _silu(y).astype(o_ref.dtype)


def _conformer(p, conv_w, conv_b, norm_g, norm_b):
    b, t, _ = p.shape
    tt = min(t, 256)
    nh = t // HALO
    cw = CONV_WIDTH
    vec = pl.BlockSpec((1, cw), lambda bi, i: (0, 0))
    return pl.pallas_call(
        functools.partial(_conformer_kernel, tt=tt),
        grid=(b, t // tt),
        in_specs=[pl.BlockSpec((1, tt, 2 * cw), lambda bi, i: (bi, i, 0)),
                  pl.BlockSpec((1, HALO, 2 * cw), lambda bi, i: (bi, jnp.maximum(i * (tt // HALO) - 1, 0), 0)),
                  pl.BlockSpec((1, HALO, 2 * cw), lambda bi, i: (bi, jnp.minimum((i + 1) * (tt // HALO), nh - 1), 0)),
                  pl.BlockSpec((CONV_KERNEL, cw), lambda bi, i: (0, 0)),
                  vec, vec, vec],
        out_specs=pl.BlockSpec((1, tt, cw), lambda bi, i: (bi, i, 0)),
        out_shape=jax.ShapeDtypeStruct((b, t, cw), BF16),
        scratch_shapes=[pltpu.VMEM((tt + 2 * HALO, cw), F32)],
        compiler_params=_cparams("parallel", "parallel"),
        name="conformer",
    )(p, p, p, conv_w, conv_b.reshape(1, cw), norm_g.reshape(1, cw), norm_b.reshape(1, cw))


def _head_ones(n, width):
    r = lax.broadcasted_iota(jnp.int32, (n, n), 0) // width
    c = lax.broadcasted_iota(jnp.int32, (n, n), 1) // width
    return jnp.where(r == c, 1.0, 0.0).astype(BF16)


def _softplus(x):
    return jnp.maximum(x, 0.0) + jnp.log(1.0 + jnp.exp(-jnp.abs(x)))


def _rwkv_feat_kernel(z_ref, zp_ref, zn_ref, mu_ref, w0_ref, w2_ref, a0_ref, a2_ref, g2_ref, kk_ref, ka_ref,
                      r_o, v_o, kk_o, lw0_o, lw1_o, kd0_o, kd1_o, b0_o, b1_o, gate_o, zs, *, tt):
    i, n = pl.program_id(1), pl.num_programs(1)
    zs[0:HALO] = jnp.where(i > 0, zp_ref[0], 0.0)
    zs[HALO:HALO + tt] = z_ref[0]
    zs[HALO + tt:] = jnp.where(i < n - 1, zn_ref[0], 0.0)
    z = z_ref[0]
    z_prev = zs[HALO - 1:HALO - 1 + tt, :]
    z_next = zs[HALO + 1:HALO + 1 + tt, :]
    x = z + mu_ref[0:1, :] * (z_prev - z) + mu_ref[1:2, :] * (z_next - z)
    w = RWKV_WIDTH
    r, k, v = x[:, :w], x[:, w:2 * w], x[:, 2 * w:3 * w]
    lora = x[:, 3 * w:3 * w + LORA_IN]
    gate_in = x[:, 3 * w + LORA_IN:]
    r_o[0] = r
    v_o[0] = v
    kk = k * kk_ref[...]
    ones = _head_ones(LANES, RWKV_HEAD)
    sq = kk * kk
    ssum = jnp.concatenate([_split_dot(sq[:, c * LANES:(c + 1) * LANES], ones, 2) for c in range(w // LANES)], axis=-1)
    kk = kk * lax.rsqrt(ssum + 1e-12)
    kk_o[0] = kk
    lora_t = jnp.tanh(lora).astype(BF16)
    lora_b = lora.astype(BF16)
    for d, (lw_o, kd_o, b_o) in enumerate(((lw0_o, kd0_o, b0_o), (lw1_o, kd1_o, b1_o))):
        wl = -_softplus(-(w0_ref[d:d + 1, :] + _dot(lora_t, w2_ref[d]))) - 0.5
        lw_o[0] = -jnp.exp(wl)
        a = jax.nn.sigmoid(a0_ref[d:d + 1, :] + _dot(lora_b, a2_ref[d]))
        kd_o[0] = k * (1.0 + (a - 1.0) * ka_ref[...])
        b_o[0] = kk * a
    gate_o[0] = _dot(jax.nn.sigmoid(gate_in).astype(BF16), g2_ref[...])


def _rwkv_features(z, mu, w0, w2p, a0, a2p, g2p, k_k, k_a):
    b, t, zw = z.shape
    tt = min(t, 256)
    nh = t // HALO
    w = RWKV_WIDTH
    vec = pl.BlockSpec((1, w), lambda bi, i: (0, 0))
    out_spec = pl.BlockSpec((1, tt, w), lambda bi, i: (bi, i, 0))
    out = jax.ShapeDtypeStruct((b, t, w), F32)
    return pl.pallas_call(
        functools.partial(_rwkv_feat_kernel, tt=tt),
        grid=(b, t // tt),
        in_specs=[pl.BlockSpec((1, tt, zw), lambda bi, i: (bi, i, 0)),
                  pl.BlockSpec((1, HALO, zw), lambda bi, i: (bi, jnp.maximum(i * (tt // HALO) - 1, 0), 0)),
                  pl.BlockSpec((1, HALO, zw), lambda bi, i: (bi, jnp.minimum((i + 1) * (tt // HALO), nh - 1), 0)),
                  pl.BlockSpec((2, zw), lambda bi, i: (0, 0)),
                  pl.BlockSpec((2, w), lambda bi, i: (0, 0)),
                  pl.BlockSpec((2, LORA_IN, w), lambda bi, i: (0, 0, 0)),
                  pl.BlockSpec((2, w), lambda bi, i: (0, 0)),
                  pl.BlockSpec((2, LORA_IN, w), lambda bi, i: (0, 0, 0)),
                  pl.BlockSpec((GATE_PAD, w), lambda bi, i: (0, 0)),
                  vec, vec],
        out_specs=[out_spec] * 10,
        out_shape=[out] * 10,
        scratch_shapes=[pltpu.VMEM((tt + 2 * HALO, zw), F32)],
        compiler_params=_cparams("parallel", "parallel"),
        name="rwkv_features",
    )(z, z, z, mu, w0, w2p, a0, a2p, g2p, k_k.reshape(1, w), k_a.reshape(1, w))


def _rwkv_chunk_terms(chains, want_y, between=()):
    pending = list(between)

    def tick():
        if pending:
            pending.pop(0)()

    n = len(chains)
    rng = range(n)
    c = chains[0][1].shape[0]
    low_lane = chains[0][6][3]
    block_mask = chains[0][6][4]
    r, v, kk, lw, kd, beta = ([ch[k] for ch in chains] for k in range(6))
    tri_incl, incl2, strict2 = ([ch[6][k] for ch in chains] for k in range(3))

    def stack_heads(x):
        return jnp.concatenate([jnp.where(low_lane, x, 0.0), jnp.where(low_lane, 0.0, x)], axis=0)

    def unstack(x):
        return jnp.where(low_lane, x[:c], x[c:])

    cs = [_ones_dot(tri_incl[i], lw[i], 3) for i in rng]
    tick()
    tot = [jnp.sum(lw[i], axis=0, keepdims=True) for i in rng]
    e_neg = [jnp.exp(-cs[i]) for i in rng]
    e_rem = [jnp.exp(tot[i] - cs[i]) for i in rng]
    kt_st = [stack_heads(kk[i] * jnp.exp(cs[i] - lw[i])) for i in rng]
    kt_sb = [kt_st[i].astype(BF16) for i in rng]
    bh_sb = [stack_heads(beta[i] * e_neg[i]).astype(BF16) for i in rng]
    kh_sb = [stack_heads(kd[i] * e_neg[i]).astype(BF16) for i in rng]
    right = [jnp.concatenate([beta[i] * e_rem[i], kd[i] * e_rem[i]], axis=0).astype(BF16) for i in rng]
    v_b = [v[i].astype(BF16) for i in rng]
    v2 = [jnp.concatenate([v_b[i], v_b[i]], axis=0) for i in rng]
    a_kb = [_dot_nt(kt_sb[i], bh_sb[i]) * strict2[i] for i in rng]
    a_kk = [(_dot_nt(kt_sb[i], kh_sb[i]) * strict2[i]).astype(BF16) for i in rng]
    tick()
    if want_y:
        rt = [r[i] * jnp.exp(cs[i]) for i in rng]
        rt_sb = [stack_heads(rt[i]).astype(BF16) for i in rng]
        a_rb = [(_dot_nt(rt_sb[i], bh_sb[i]) * incl2[i]).astype(BF16) for i in rng]
        a_rk = [(_dot_nt(rt_sb[i], kh_sb[i]) * incl2[i]).astype(BF16) for i in rng]
        tick()
    rhs0 = [_dot(a_kk[i], v2[i]) for i in rng]
    tick()
    p = [-a_kb[i] for i in rng]
    x = list(p)
    for _ in range(int(np.log2(c)) - 1):
        p_b = [p[i].astype(BF16) for i in rng]
        x_b = [x[i].astype(BF16) for i in rng]
        p = [_dot(p_b[i], p_b[i]) for i in rng]
        xp = [_dot(x_b[i], p[i].astype(BF16)) for i in rng]
        x = [x[i] + p[i] + xp[i] for i in rng]
        tick()
    both = [_dot(x[i].astype(BF16), jnp.concatenate([rhs0[i].astype(BF16), kt_sb[i]], axis=1)) for i in rng]
    u0 = [unstack(-(rhs0[i] + both[i][:, :LANES])) for i in rng]
    kq_st = [kt_st[i] + both[i][:, LANES:] for i in rng]
    kq = [unstack(kq_st[i]) for i in rng]
    lhs_t = [jnp.concatenate([jnp.concatenate([u0[i], v[i]], axis=0).T,
                              jnp.concatenate([kq[i], jnp.zeros_like(kq[i])], axis=0).T], axis=0).astype(BF16)
             for i in rng]
    tick()
    gq = [_dot(lhs_t[i], right[i]) for i in rng]
    tick()
    if want_y:
        u0_b = [u0[i].astype(BF16) for i in rng]
        y0 = [_dot(jnp.concatenate([a_rb[i], a_rk[i]], axis=1), jnp.concatenate([u0_b[i], u0_b[i], v2[i]], axis=0))
              for i in rng]
        rq = [_dot(a_rb[i], kq_st[i].astype(BF16)) for i in rng]
    while pending:
        tick()
    out = []
    for i in rng:
        g_add = gq[i][:LANES] * block_mask
        q_mat = (gq[i][LANES:] * block_mask).astype(BF16)
        if want_y:
            out.append((jnp.exp(tot[i]), q_mat, g_add, (rt[i] - unstack(rq[i])).astype(BF16), unstack(y0[i])))
        else:
            out.append((jnp.exp(tot[i]), q_mat, g_add, None, None))
    return out


def _rwkv_consts(c, reverse):
    t = lax.broadcasted_iota(jnp.int32, (c, c), 0)
    s = lax.broadcasted_iota(jnp.int32, (c, c), 1)
    incl = (s >= t) if reverse else (s <= t)
    tri_incl = jnp.where(incl, 1.0, 0.0).astype(BF16)
    t2 = lax.broadcasted_iota(jnp.int32, (2 * c, 2 * c), 0)
    s2 = lax.broadcasted_iota(jnp.int32, (2 * c, 2 * c), 1)
    same = (t2 // c) == (s2 // c)
    order = (s2 >= t2) if reverse else (s2 <= t2)
    incl2 = jnp.where(same & order, 1.0, 0.0)
    strict2 = jnp.where(same & order & (s2 != t2), 1.0, 0.0)
    low_lane = lax.broadcasted_iota(jnp.int32, (1, LANES), 1) < RWKV_HEAD
    bi = lax.broadcasted_iota(jnp.int32, (LANES, LANES), 0) // RWKV_HEAD
    bj = lax.broadcasted_iota(jnp.int32, (LANES, LANES), 1) // RWKV_HEAD
    block_mask = jnp.where(bi == bj, 1.0, 0.0)
    return tri_incl, incl2, strict2, low_lane, block_mask


def _rwkv_kernel(r_l, v_l, kk_l, lw0_l, lw1_l, kd0_l, kd1_l, b0_l, b1_l, gate_l,
                 v_c, kk_c, lw0_c, lw1_c, kd0_c, kd1_c, b0_c, b1_c,
                 rk_ref, lng_ref, lnb_ref, o_ref, acc_s, gam_s, q_s, g_s, re_s):
    c = RWKV_CHUNK
    n_l, n_c = r_l.shape[1] // c, v_c.shape[1] // c
    consts = (_rwkv_consts(c, False), _rwkv_consts(c, True))
    lat_dirs = ((lw0_l, kd0_l, b0_l), (lw1_l, kd1_l, b1_l))
    ctx_dirs = ((lw0_c, kd0_c, b0_c), (lw1_c, kd1_c, b1_c))
    acc_s[...] = jnp.zeros_like(acc_s)
    state = [jnp.zeros((LANES, LANES), F32), jnp.zeros((LANES, LANES), F32)]

    def chunk_rows(ci):
        return slice(ci * c, (ci + 1) * c)

    def scan_step(slot_f, slot_b, lat_f=None, lat_b=None):
        def step():
            nxt = []
            for d, slot, lat in ((0, slot_f, lat_f), (1, slot_b, lat_b)):
                s_in = state[d]
                s_b = s_in.astype(BF16)
                nxt.append(s_in * gam_s[d, slot] - _dot(s_b, q_s[d, slot]) + g_s[d, slot])
                if lat is not None:
                    acc_s[chunk_rows(lat), :] += _dot_nt(re_s[d, lat], s_b)
            state[:] = nxt
        return step

    def build_terms(groups, refs, dirs, slot0, want_y, between):
        r_ref, v_ref, kk_ref = refs
        chains, where = [], []
        for d, (lw_r, kd_r, b_r) in enumerate(dirs):
            for ci in groups[d]:
                rows = chunk_rows(ci)
                r = r_ref[0, rows, :] if want_y else None
                chains.append((r, v_ref[0, rows, :], kk_ref[0, rows, :], lw_r[0, rows, :], kd_r[0, rows, :],
                               b_r[0, rows, :], consts[d]))
                where.append((d, ci))
        terms = _rwkv_chunk_terms(chains, want_y, between)
        for (d, ci), (gam, q_mat, g_add, r_eff, y0) in zip(where, terms):
            gam_s[d, slot0 + ci] = gam
            q_s[d, slot0 + ci] = q_mat
            g_s[d, slot0 + ci] = g_add
            if want_y:
                re_s[d, ci] = r_eff
                acc_s[chunk_rows(ci), :] += y0

    group = min(RWKV_GROUP, n_l)
    n_groups = n_l // group
    build_terms((range(n_c), range(n_c)), (None, v_c, kk_c), ctx_dirs, 0, False, ())
    steps = [scan_step(k, n_c - 1 - k) for k in range(n_c)]
    for gi in range(n_groups):
        gb = n_groups - 1 - gi
        build_terms((range(gi * group, (gi + 1) * group), range(gb * group, (gb + 1) * group)),
                    (r_l, v_l, kk_l), lat_dirs, n_c, True, steps)
        steps = [scan_step(n_c + gi * group + k, n_c + (gb + 1) * group - 1 - k,
                           gi * group + k, (gb + 1) * group - 1 - k) for k in range(group)]
    for step in steps:
        step()

    ones = _head_ones(LANES, RWKV_HEAD)
    wkv = acc_s[...]
    inv_n = 1.0 / RWKV_HEAD
    mean = _split_dot(wkv, ones, 2) * inv_n
    xc = wkv - mean
    var = _split_dot(xc * xc, ones, 2) * inv_n
    y = xc * lax.rsqrt(var + RWKV_GN_EPS) * lng_ref[...] + lnb_ref[...]
    r = r_l[0]
    rk = rk_ref[...]
    bonus = (_split_dot(r * kd0_l[0] * rk, ones, 2) + _split_dot(r * kd1_l[0] * rk, ones, 2)) * v_l[0]
    o_ref[0] = ((y + bonus) * gate_l[0]).astype(o_ref.dtype)


def _rwkv_mix(feat_l, feat_c, r_k, ln_g, ln_b):
    r_l, v_l, kk_l, lw0_l, lw1_l, kd0_l, kd1_l, b0_l, b1_l, gate_l = feat_l
    _, v_c, kk_c, lw0_c, lw1_c, kd0_c, kd1_c, b0_c, b1_c, _ = feat_c
    b, t, w = r_l.shape
    t_c = v_c.shape[1]
    n_slots = (t_c + t) // RWKV_CHUNK
    lat = pl.BlockSpec((1, t, LANES), lambda bi, p: (bi, 0, p))
    ctx = pl.BlockSpec((1, t_c, LANES), lambda bi, p: (bi, 0, p))
    vec = pl.BlockSpec((1, LANES), lambda bi, p: (0, p))
    return pl.pallas_call(
        _rwkv_kernel,
        grid=(b, w // LANES),
        in_specs=[lat] * 10 + [ctx] * 8 + [vec] * 3,
        out_specs=lat,
        out_shape=jax.ShapeDtypeStruct((b, t, w), BF16),
        scratch_shapes=[pltpu.VMEM((t, LANES), F32),
                        pltpu.VMEM((2, n_slots, 1, LANES), F32),
                        pltpu.VMEM((2, n_slots, LANES, LANES), BF16),
                        pltpu.VMEM((2, n_slots, LANES, LANES), F32),
                        pltpu.VMEM((2, t // RWKV_CHUNK, RWKV_CHUNK, LANES), BF16)],
        compiler_params=_cparams("parallel", "parallel"),
        name="rwkv_mix",
    )(r_l, v_l, kk_l, lw0_l, lw1_l, kd0_l, kd1_l, b0_l, b1_l, gate_l,
      v_c, kk_c, lw0_c, lw1_c, kd0_c, kd1_c, b0_c, b1_c,
      r_k.reshape(1, w), ln_g.reshape(1, w), ln_b.reshape(1, w))


def _retention_kernel(dl_ref, q_l, k_l, v_l, rg_l, k_c, v_c, cos_ref, sin_ref, gg_ref, gb_ref, o_ref,
                      q_s, k_s, kv_s, rin_s):
    c = RET_CHUNK
    n_l, n_c = q_l.shape[1] // c, k_c.shape[1] // c
    h = pl.program_id(1)
    k_scale = HEAD_DIM ** -0.5
    q_s[...] = _rope(q_l[0].astype(F32), cos_ref[...], sin_ref[...]).astype(BF16)
    k_s[...] = (_rope(k_l[0].astype(F32), cos_ref[...], sin_ref[...]) * k_scale).astype(BF16)
    ti = lax.broadcasted_iota(jnp.int32, (c, c), 0).astype(F32)
    si = lax.broadcasted_iota(jnp.int32, (c, c), 1).astype(F32)
    col = lax.broadcasted_iota(jnp.int32, (c, 1), 0).astype(F32)
    row = lax.broadcasted_iota(jnp.int32, (1, c), 1).astype(F32)
    lg = [jax.nn.log_sigmoid(jnp.full((1, LANES), dl_ref[d, h], F32))[:, :1] for d in range(2)]
    fwd = jnp.where(ti >= si, jnp.exp(lg[0] * jnp.maximum(ti - si, 0.0)), 0.0)
    bwd = jnp.where(si >= ti, jnp.exp(lg[1] * jnp.maximum(si - ti, 0.0)), 0.0)
    inner = fwd + bwd
    q_decay = (jnp.exp(lg[0] * (col + 1.0)), jnp.exp(lg[1] * (c - col)))
    k_decay = (jnp.exp(lg[0] * (c - 1.0 - row)), jnp.exp(lg[1] * row))
    chunk_decay = (jnp.exp(lg[0] * c), jnp.exp(lg[1] * c))

    def kv_body(ii, carry, *, group, slot0, lat):
        ks, vs, slots = [], [], []
        for g in range(group):
            ci = ii * group + g
            rows = pl.ds(pl.multiple_of(ci * c, c), c)
            ks.append(k_s[rows, :].astype(F32) if lat else k_c[0, rows, :].astype(F32) * k_scale)
            vs.append(v_l[0, rows, :] if lat else v_c[0, rows, :])
            slots.append(slot0 + ci)
        kts = [k.T for k in ks]
        lhs = [jnp.concatenate([kt * k_decay[0], kt * k_decay[1]], axis=0).astype(BF16) for kt in kts]
        kv = [_dot(a, v) for a, v in zip(lhs, vs)]
        for slot, x in zip(slots, kv):
            kv_s[0, slot] = x[:HEAD_DIM]
            kv_s[1, slot] = x[HEAD_DIM:]
        return carry

    g_c = min(RET_GROUP, n_c)
    lax.fori_loop(0, n_c // g_c, functools.partial(kv_body, group=g_c, slot0=0, lat=False), 0)
    lax.fori_loop(0, n_l // RET_GROUP, functools.partial(kv_body, group=RET_GROUP, slot0=n_c, lat=True), 0)

    def scan_body(k, carry, *, n, slot0, record):
        r_f, r_b = carry
        kb = n - 1 - k
        if record:
            rin_s[0, k] = r_f.astype(BF16)
            rin_s[1, kb] = r_b.astype(BF16)
        return r_f * chunk_decay[0] + kv_s[0, slot0 + k], r_b * chunk_decay[1] + kv_s[1, slot0 + kb]

    zero = jnp.zeros((HEAD_DIM, RET_V_DIM), F32)
    carry = lax.fori_loop(0, n_c, functools.partial(scan_body, n=n_c, slot0=0, record=False), (zero, zero))
    lax.fori_loop(0, n_l, functools.partial(scan_body, n=n_l, slot0=n_c, record=True), carry)

    def out_body(ii, carry):
        qs, kbs, vs, rows_l = [], [], [], []
        for g in range(RET_GROUP):
            ci = ii * RET_GROUP + g
            rows = pl.ds(pl.multiple_of(ci * c, c), c)
            rows_l.append((ci, rows))
            qs.append(q_s[rows, :])
            kbs.append(k_s[rows, :])
            vs.append(v_l[0, rows, :])
        s = [(_dot_nt(q, k) * inner).astype(BF16) for q, k in zip(qs, kbs)]
        qd = []
        for q in qs:
            qf = q.astype(F32)
            qd.append(jnp.concatenate([qf * q_decay[0], qf * q_decay[1]], axis=1).astype(BF16))
        r_in = [jnp.concatenate([rin_s[0, ci], rin_s[1, ci]], axis=0) for ci, _ in rows_l]
        o = [_dot(s[g], vs[g]) + _dot(qd[g], r_in[g]) for g in range(RET_GROUP)]
        for (ci, rows), x in zip(rows_l, o):
            xc = x - jnp.mean(x, axis=-1, keepdims=True)
            var = jnp.mean(xc * xc, axis=-1, keepdims=True)
            y = xc * lax.rsqrt(var + NORM_EPS) * gg_ref[...] + gb_ref[...]
            o_ref[0, rows, :] = (y * _silu(rg_l[0, rows, :].astype(F32))).astype(o_ref.dtype)
        return carry

    lax.fori_loop(0, n_l // RET_GROUP, out_body, 0)


def _retention(p_l, p_c, decay_logit, gn_g, gn_b, cos_t, sin_t):
    b, t, _ = p_l.shape
    t_c = p_c.shape[1]
    nh, dk, dv = RET_HEADS, HEAD_DIM, RET_V_DIM
    k_blk, v_blk, g_blk = nh, (2 * nh * dk) // dv, (2 * nh * dk + nh * dv) // dv
    return pl.pallas_call(
        _retention_kernel,
        grid=(b, nh),
        in_specs=[pl.BlockSpec(memory_space=pltpu.SMEM),
                  pl.BlockSpec((1, t, dk), lambda bi, h: (bi, 0, h)),
                  pl.BlockSpec((1, t, dk), lambda bi, h: (bi, 0, k_blk + h)),
                  pl.BlockSpec((1, t, dv), lambda bi, h: (bi, 0, v_blk + h)),
                  pl.BlockSpec((1, t, dv), lambda bi, h: (bi, 0, g_blk + h)),
                  pl.BlockSpec((1, t_c, dk), lambda bi, h: (bi, 0, k_blk + h)),
                  pl.BlockSpec((1, t_c, dv), lambda bi, h: (bi, 0, v_blk + h)),
                  pl.BlockSpec((t, dk), lambda bi, h: (0, 0)),
                  pl.BlockSpec((t, dk), lambda bi, h: (0, 0)),
                  pl.BlockSpec((1, dv), lambda bi, h: (0, h)),
                  pl.BlockSpec((1, dv), lambda bi, h: (0, h))],
        out_specs=pl.BlockSpec((1, t, dv), lambda bi, h: (bi, 0, h)),
        out_shape=jax.ShapeDtypeStruct((b, t, nh * dv), BF16),
        scratch_shapes=[pltpu.VMEM((t, dk), BF16), pltpu.VMEM((t, dk), BF16),
                        pltpu.VMEM((2, (t_c + t) // RET_CHUNK, dk, dv), F32),
                        pltpu.VMEM((2, t // RET_CHUNK, dk, dv), BF16)],
        compiler_params=_cparams("parallel", "parallel"),
        name="retention",
    )(decay_logit, p_l, p_l, p_l, p_l, p_c, p_c, cos_t, sin_t, gn_g.reshape(1, nh * dv), gn_b.reshape(1, nh * dv))


def kernel(x, c, ctx, c_ctx, mod_w, mod_b, norm_mix_g, norm_ffn_g, ffn_w_up, ffn_conv_w, ffn_conv_b, ffn_w_down, ab_w_in, ab_q_norm, ab_k_norm, ab_conv_w, ab_conv_b, ab_conv_norm_g, ab_conv_norm_b, ab_w_out, cd_w_in, cd_shift_mu, rwkv_w0, rwkv_w2, rwkv_a0, rwkv_a2, rwkv_g2, rwkv_k_k, rwkv_k_a, rwkv_r_k, rwkv_ln_g, rwkv_ln_b, ret_decay_logit, ret_gn_g, ret_gn_b, cd_w_out, final_norm_g):
    b, t, d = x.shape
    t_c = ctx.shape[1]
    att_w = ATT_HEADS * HEAD_DIM
    kv_w = att_w // ATT_GROUP

    pad_rows = (-(b + 1)) % 8
    c_all = jnp.concatenate([c, c_ctx[None, :], jnp.zeros((pad_rows, d), F32)], axis=0)
    mod = _modulation(c_all, mod_w, mod_b)

    def mod_vectors(layer):
        lat = [m.reshape(b, 1, d) for m in jnp.split(mod[layer, :b], 6, axis=-1)]
        cx = [m.reshape(1, 1, d) for m in jnp.split(mod[layer, b:b + 1], 6, axis=-1)]
        return lat, cx

    cos_t, sin_t = _rope_tables(t)
    h = x
    hc = ctx.reshape(1, b * t_c, d)

    (sh1, sc1, g1, sh2, sc2, g2), (csh1, csc1, cg1, csh2, csc2, cg2) = mod_vectors(0)
    w_in = ab_w_in[0]
    glu0 = att_w + 2 * kv_w
    w_in = jnp.concatenate([w_in[:, glu0:], w_in[:, :glu0]], axis=1).astype(BF16)
    p_l = _norm_mod_matmul(h, norm_mix_g[0], sh1, sc1, w_in, BF16, "ab_in_lat")
    p_c = _norm_mod_matmul(hc, norm_mix_g[0], csh1, csc1, w_in, BF16, "ab_in_ctx").reshape(b, t_c, -1)
    att_l = _attention(p_l, p_c, p_l, ab_q_norm[0], ab_k_norm[0], cos_t, sin_t)
    att_c = _attention(p_c, p_c, None, ab_q_norm[0], ab_k_norm[0], None, None)
    cv_l = _conformer(p_l, ab_conv_w[0], ab_conv_b[0], ab_conv_norm_g[0], ab_conv_norm_b[0])
    cv_c = _conformer(p_c, ab_conv_w[0], ab_conv_b[0], ab_conv_norm_g[0], ab_conv_norm_b[0])
    w_out = ab_w_out[0].astype(BF16)
    w_o = [w_out[:att_w], w_out[att_w:]]
    h = _matmul_gated_residual([att_l, cv_l], w_o, h, g1, "ab_out_lat")
    hc = _matmul_gated_residual([att_c.reshape(1, b * t_c, -1), cv_c.reshape(1, b * t_c, -1)], w_o, hc, cg1,
                                "ab_out_ctx")
    w_up, w_down = ffn_w_up.astype(BF16), ffn_w_down.astype(BF16)
    h = _conv_ffn(h, norm_ffn_g[0], sh2, sc2, g2, w_up, ffn_conv_w[0], ffn_conv_b[0], w_down, 0, final_norm_g,
                  t, False, "ffn0_lat")
    hc = _conv_ffn(hc, norm_ffn_g[0], csh2, csc2, cg2, w_up, ffn_conv_w[0], ffn_conv_b[0], w_down, 0,
                   final_norm_g, t_c, False, "ffn0_ctx")

    (sh1, sc1, g1, sh2, sc2, g2), (csh1, csc1, _, _, _, _) = mod_vectors(1)
    rw_in = 3 * RWKV_WIDTH + 2 * RWKV_HEAD + GATE_LORA
    zpad = RWKV_IN_PAD - rw_in
    w_z = jnp.pad(cd_w_in[0][:, :rw_in], ((0, 0), (0, zpad))).astype(BF16)
    w_ret = cd_w_in[0][:, rw_in:].astype(BF16)
    z_l = _norm_mod_matmul(h, norm_mix_g[1], sh1, sc1, w_z, F32, "cd_in_z_lat")
    z_c = _norm_mod_matmul(hc, norm_mix_g[1], csh1, csc1, w_z, F32, "cd_in_z_ctx").reshape(b, t_c, -1)
    rp_l = _norm_mod_matmul(h, norm_mix_g[1], sh1, sc1, w_ret, BF16, "cd_in_ret_lat")
    rp_c = _norm_mod_matmul(hc, norm_mix_g[1], csh1, csc1, w_ret, BF16, "cd_in_ret_ctx").reshape(b, t_c, -1)

    mu = jnp.pad(cd_shift_mu[0], ((0, 0), (0, zpad)))
    lora_pad = LORA_IN - RWKV_HEAD
    w2p = jnp.pad(rwkv_w2[0], ((0, 0), (0, lora_pad), (0, 0))).astype(BF16)
    a2p = jnp.pad(rwkv_a2[0], ((0, 0), (lora_pad, 0), (0, 0))).astype(BF16)
    g2p = jnp.pad(rwkv_g2[0], ((0, GATE_PAD - GATE_LORA), (0, 0))).astype(BF16)
    feat_l = _rwkv_features(z_l, mu, rwkv_w0[0], w2p, rwkv_a0[0], a2p, g2p, rwkv_k_k[0], rwkv_k_a[0])
    feat_c = _rwkv_features(z_c, mu, rwkv_w0[0], w2p, rwkv_a0[0], a2p, g2p, rwkv_k_k[0], rwkv_k_a[0])
    y_c = _rwkv_mix(feat_l, feat_c, rwkv_r_k[0], rwkv_ln_g[0], rwkv_ln_b[0])
    y_d = _retention(rp_l, rp_c, ret_decay_logit[0], ret_gn_g[0], ret_gn_b[0], cos_t, sin_t)
    w_out = cd_w_out[0].astype(BF16)
    h = _matmul_gated_residual([y_c, y_d], [w_out[:RWKV_WIDTH], w_out[RWKV_WIDTH:]], h, g1, "cd_out_lat")
    return _conv_ffn(h, norm_ffn_g[1], sh2, sc2, g2, w_up, ffn_conv_w[1], ffn_conv_b[1], w_down, 1, final_norm_g,
                     t, True, "ffn1_lat")
```

```python
import functools

import jax
import jax.numpy as jnp
import numpy as np
from jax import lax
from jax.experimental import pallas as pl
from jax.experimental.pallas import tpu as pltpu

F32 = jnp.float32
BF16 = jnp.bfloat16

NORM_EPS = 1e-6
GRID_W = 64
ROPE_BASE = 10000.0
HEAD_DIM = 128
ATT_HEADS = 8
ATT_GROUP = 4
CONV_WIDTH = 1024
CONV_KERNEL = 31
RWKV_WIDTH = 1024
RWKV_HEAD = 64
RWKV_GN_EPS = 64e-5
RWKV_CHUNK = 64
RWKV_GROUP = 8
LORA_IN = 128
GATE_LORA = 160
GATE_PAD = 384
RWKV_IN_PAD = 3584
RET_HEADS = 8
RET_V_DIM = 256
RET_CHUNK = 128
RET_GROUP = 4
FFN_DIM = 5632
HALO = 16
NORM_PIECE = 16
FFN_ROW_TILE = 1024
FFN_ACC_COLS = 512
FFN_OUT_ROWS = 64
INPROJ_ROW_CHUNK = 256
LANES = 128
SUBLANES = 8
LOG2_E = 1.4426950408889634

VMEM_LIMIT_BYTES = 52 * 1024 * 1024


def _cparams(*sem):
    return pltpu.CompilerParams(dimension_semantics=sem, vmem_limit_bytes=VMEM_LIMIT_BYTES)


def _dot(a, b):
    return jnp.dot(a, b, preferred_element_type=F32)


def _dot_nt(a, b):
    return lax.dot_general(a, b, (((1,), (1,)), ((), ())), preferred_element_type=F32)


def _split_dot(x, ones_bf16, terms):
    acc = None
    rem = x
    for _ in range(terms):
        piece = rem.astype(BF16)
        part = _dot(piece, ones_bf16)
        acc = part if acc is None else acc + part
        rem = rem - piece.astype(F32)
    return acc


def _ones_dot(ones_bf16, x, terms):
    acc = None
    rem = x
    for _ in range(terms):
        piece = rem.astype(BF16)
        part = _dot(ones_bf16, piece)
        acc = part if acc is None else acc + part
        rem = rem - piece.astype(F32)
    return acc


def _rms(x, g):
    return x * lax.rsqrt(jnp.mean(x * x, axis=-1, keepdims=True) + NORM_EPS) * g


def _silu(x):
    return x * jax.nn.sigmoid(x)


def _rope(x, cos, sin_signed):
    n = x.shape[-1]
    lane = lax.broadcasted_iota(jnp.int32, x.shape, x.ndim - 1)
    partner = jnp.where((lane % 64) < 32, pltpu.roll(x, n - 32, x.ndim - 1), pltpu.roll(x, 32, x.ndim - 1))
    return x * cos + partner * sin_signed


def _rope_tables(n_tok):
    rows = n_tok // GRID_W
    row = jnp.repeat(jnp.arange(rows, dtype=F32), GRID_W)
    col = jnp.tile(jnp.arange(GRID_W, dtype=F32), rows)
    axis_dim = HEAD_DIM // 2
    inv_freq = ROPE_BASE ** (-jnp.arange(0, axis_dim, 2, dtype=F32) / axis_dim)
    ang = jnp.concatenate([row[:, None] * inv_freq, col[:, None] * inv_freq], axis=-1)
    cos, sin = jnp.cos(ang), jnp.sin(ang)
    c_r, c_c, s_r, s_c = cos[:, :32], cos[:, 32:], sin[:, :32], sin[:, 32:]
    cos_t = jnp.concatenate([c_r, c_r, c_c, c_c], axis=-1)
    sin_t = jnp.concatenate([-s_r, s_r, -s_c, s_c], axis=-1)
    return cos_t, sin_t


def _mod_kernel(c_ref, w_ref, b_ref, o_ref):
    s = _silu(c_ref[...]).astype(BF16)
    o_ref[0] = _dot(s, w_ref[0].astype(BF16)) + b_ref[0]


def _modulation(c_all, mod_w, mod_b):
    n_layers, d, n = mod_w.shape
    rows = c_all.shape[0]
    tn = 1024
    return pl.pallas_call(
        _mod_kernel,
        grid=(n_layers, n // tn),
        in_specs=[pl.BlockSpec((rows, d), lambda l, j: (0, 0)),
                  pl.BlockSpec((1, d, tn), lambda l, j: (l, 0, j)),
                  pl.BlockSpec((1, 1, tn), lambda l, j: (l, 0, j))],
        out_specs=pl.BlockSpec((1, rows, tn), lambda l, j: (l, 0, j)),
        out_shape=jax.ShapeDtypeStruct((n_layers, rows, n), F32),
        compiler_params=_cparams("parallel", "parallel"),
        name="modulation",
    )(c_all, mod_w, mod_b.reshape(n_layers, 1, n))


def _norm_mod_rows(src, dst, dst_row0, n_rows, gain, shift):
    for r0 in range(0, n_rows, NORM_PIECE):
        x = src(r0)
        inv = lax.rsqrt(jnp.mean(x * x, axis=-1, keepdims=True) + NORM_EPS)
        dst[dst_row0 + r0:dst_row0 + r0 + NORM_PIECE, :] = (x * inv * gain + shift).astype(dst.dtype)


def _inproj_kernel(h_ref, g_ref, sh_ref, sc_ref, w_ref, o_ref, u_s, *, tm):
    j = pl.program_id(2)

    @pl.when(j == 0)
    def _():
        gain = g_ref[...] * (1.0 + sc_ref[0])
        shift = sh_ref[0]
        chunk = min(tm, INPROJ_ROW_CHUNK)
        for c0 in range(0, tm, chunk):
            _norm_mod_rows(lambda r0: h_ref[0, c0 + r0:c0 + r0 + NORM_PIECE, :], u_s, c0, chunk, gain, shift)
            o_ref[0, c0:c0 + chunk, :] = _dot(u_s[c0:c0 + chunk, :], w_ref[...]).astype(o_ref.dtype)

    @pl.when(j > 0)
    def _():
        o_ref[0] = _dot(u_s[...], w_ref[...]).astype(o_ref.dtype)


def _norm_mod_matmul(h, g, shift, scale, w, out_dtype, name):
    bm, t, d = h.shape
    n = w.shape[1]
    tm = min(t, 1024)
    tn = 512
    return pl.pallas_call(
        functools.partial(_inproj_kernel, tm=tm),
        grid=(bm, t // tm, n // tn),
        in_specs=[pl.BlockSpec((1, tm, d), lambda b, i, j: (b, i, 0)),
                  pl.BlockSpec((1, d), lambda b, i, j: (0, 0)),
                  pl.BlockSpec((1, 1, d), lambda b, i, j: (b, 0, 0)),
                  pl.BlockSpec((1, 1, d), lambda b, i, j: (b, 0, 0)),
                  pl.BlockSpec((d, tn), lambda b, i, j: (0, j))],
        out_specs=pl.BlockSpec((1, tm, tn), lambda b, i, j: (b, i, j)),
        out_shape=jax.ShapeDtypeStruct((bm, t, n), out_dtype),
        scratch_shapes=[pltpu.VMEM((tm, d), BF16)],
        compiler_params=_cparams("parallel", "parallel", "arbitrary"),
        name=name,
    )(h, g.reshape(1, d), shift, scale, w)


def _outproj_kernel(*refs, n_lhs):
    lhs, ws = refs[:n_lhs], refs[n_lhs:2 * n_lhs]
    res_ref, gt_ref, o_ref = refs[2 * n_lhs:]
    acc = _dot(lhs[0][0], ws[0][...])
    for a, w in zip(lhs[1:], ws[1:]):
        acc = acc + _dot(a[0], w[...])
    o_ref[0] = res_ref[0] + gt_ref[0] * acc


def _matmul_gated_residual(lhs_list, w_list, res, gate, name):
    bm, t, d = res.shape
    tm = min(t, 1024)
    tn = 512
    in_specs = [pl.BlockSpec((1, tm, a.shape[-1]), lambda b, i, j: (b, i, 0)) for a in lhs_list]
    in_specs += [pl.BlockSpec((w.shape[0], tn), lambda b, i, j: (0, j)) for w in w_list]
    in_specs += [pl.BlockSpec((1, tm, tn), lambda b, i, j: (b, i, j)),
                 pl.BlockSpec((1, 1, tn), lambda b, i, j: (b, 0, j))]
    return pl.pallas_call(
        functools.partial(_outproj_kernel, n_lhs=len(lhs_list)),
        grid=(bm, t // tm, d // tn),
        in_specs=in_specs,
        out_specs=pl.BlockSpec((1, tm, tn), lambda b, i, j: (b, i, j)),
        out_shape=jax.ShapeDtypeStruct((bm, t, d), F32),
        compiler_params=_cparams("parallel", "parallel", "arbitrary"),
        name=name,
    )(*lhs_list, *w_list, res, gate)


def _ffn_kernel(h_ref, hp_ref, hn_ref, g_ref, sh_ref, sc_ref, gt_ref, wg_ref, wv_ref, cw_ref, cb_ref,
                wd_ref, fg_ref, o_ref, u_s, *, tm, seq_len, final_norm):
    i, j, nj = pl.program_id(1), pl.program_id(2), pl.num_programs(2)

    @pl.when(j == 0)
    def _():
        gain = g_ref[...] * (1.0 + sc_ref[0])
        shift = sh_ref[0]
        _norm_mod_rows(lambda r0: hp_ref[0, r0:r0 + NORM_PIECE, :], u_s, 0, HALO, gain, shift)
        _norm_mod_rows(lambda r0: h_ref[0, r0:r0 + NORM_PIECE, :], u_s, HALO, tm, gain, shift)
        _norm_mod_rows(lambda r0: hn_ref[0, r0:r0 + NORM_PIECE, :], u_s, HALO + tm, HALO, gain, shift)
        o_ref[0] = jnp.zeros(o_ref.shape[1:], F32)

    rows = tm + 2 * HALO
    gate_lin = _dot(u_s[...], wg_ref[...])
    val = _dot(u_s[HALO:HALO + tm], wv_ref[...])
    pos = (i * tm + lax.broadcasted_iota(jnp.int32, (tm, 1), 0)) % seq_len
    g_prev = jnp.where(pos != 0, pltpu.roll(gate_lin, 1, 0)[HALO:HALO + tm], 0.0)
    g_next = jnp.where(pos != seq_len - 1, pltpu.roll(gate_lin, rows - 1, 0)[HALO:HALO + tm], 0.0)
    g_cur = gate_lin[HALO:HALO + tm]
    cw = cw_ref[...]
    conv = cw[0:1] * g_prev + cw[1:2] * g_cur + cw[2:3] * g_next + cb_ref[...]
    act = (_silu(conv) * val).astype(BF16)
    d = o_ref.shape[2]
    for c0 in range(0, d, FFN_ACC_COLS):
        o_ref[0, :, c0:c0 + FFN_ACC_COLS] += _dot(act, wd_ref[:, c0:c0 + FFN_ACC_COLS])

    @pl.when(j == nj - 1)
    def _():
        for r0 in range(0, tm, FFN_OUT_ROWS):
            rs = slice(r0, r0 + FFN_OUT_ROWS)
            y = h_ref[0, rs, :] + gt_ref[0] * o_ref[0, rs, :]
            if final_norm:
                y = _rms(y, fg_ref[...])
            o_ref[0, rs, :] = y


def _conv_ffn(h, g, shift, scale, gate, w_up, conv_w, conv_b, w_down, layer, final_g, seq_len, final_norm, name):
    bm, t, d = h.shape
    f = w_down.shape[1]
    tm = min(t, FFN_ROW_TILE)
    tf = 512
    nh = t // HALO
    kern = functools.partial(_ffn_kernel, tm=tm, seq_len=seq_len, final_norm=final_norm)
    vec = pl.BlockSpec((1, 1, d), lambda b, i, j: (b, 0, 0))
    return pl.pallas_call(
        kern,
        grid=(bm, t // tm, f // tf),
        in_specs=[pl.BlockSpec((1, tm, d), lambda b, i, j: (b, i, 0), pipeline_mode=pl.Buffered(1)),
                  pl.BlockSpec((1, HALO, d), lambda b, i, j: (b, jnp.maximum(i * (tm // HALO) - 1, 0), 0)),
                  pl.BlockSpec((1, HALO, d), lambda b, i, j: (b, jnp.minimum((i + 1) * (tm // HALO), nh - 1), 0)),
                  pl.BlockSpec((1, d), lambda b, i, j: (0, 0)),
                  vec, vec, vec,
                  pl.BlockSpec((None, d, tf), lambda b, i, j: (layer, 0, j)),
                  pl.BlockSpec((None, d, tf), lambda b, i, j: (layer, 0, f // tf + j)),
                  pl.BlockSpec((3, tf), lambda b, i, j: (0, j)),
                  pl.BlockSpec((1, tf), lambda b, i, j: (0, j)),
                  pl.BlockSpec((None, tf, d), lambda b, i, j: (layer, j, 0)),
                  pl.BlockSpec((1, d), lambda b, i, j: (0, 0))],
        out_specs=pl.BlockSpec((1, tm, d), lambda b, i, j: (b, i, 0)),
        out_shape=jax.ShapeDtypeStruct((bm, t, d), F32),
        scratch_shapes=[pltpu.VMEM((tm + 2 * HALO, d), BF16)],
        compiler_params=_cparams("parallel", "parallel", "arbitrary"),
        name=name,
    )(h, h, h, g.reshape(1, d), shift, scale, gate, w_up, w_up, conv_w, conv_b.reshape(1, f), w_down,
      final_g.reshape(1, d))


def _attn_kernel(*refs, has_lat):
    if has_lat:
        (q_ref, kc_ref, vc_ref, kl_ref, vl_ref, qn_ref, kn_ref, cq_ref, sq_ref, ck_ref, sk_ref,
         o_ref, kc_s, kl_s) = refs
    else:
        q_ref, kc_ref, vc_ref, qn_ref, kn_ref, o_ref, kc_s = refs

    @pl.when(pl.program_id(2) == 0)
    def _():
        kc_s[...] = _rms(kc_ref[0].astype(F32), kn_ref[...]).astype(BF16)
        if has_lat:
            kl = _rms(kl_ref[0].astype(F32), kn_ref[...])
            kl_s[...] = _rope(kl, ck_ref[...], sk_ref[...]).astype(BF16)

    scale = HEAD_DIM ** -0.5 * LOG2_E
    for hh in range(ATT_GROUP):
        cols = slice(hh * HEAD_DIM, (hh + 1) * HEAD_DIM)
        q = _rms(q_ref[0, :, cols].astype(F32), qn_ref[...])
        if has_lat:
            q = _rope(q, cq_ref[...], sq_ref[...])
        q = (q * scale).astype(BF16)
        s_c = _dot_nt(q, kc_s[...])
        m = jnp.max(s_c, axis=-1, keepdims=True)
        if has_lat:
            s_l = _dot_nt(q, kl_s[...])
            m = jnp.maximum(m, jnp.max(s_l, axis=-1, keepdims=True))
        p_c = jnp.exp2(s_c - m)
        denom = jnp.sum(p_c, axis=-1, keepdims=True)
        acc = _dot(p_c.astype(BF16), vc_ref[0])
        if has_lat:
            p_l = jnp.exp2(s_l - m)
            denom = denom + jnp.sum(p_l, axis=-1, keepdims=True)
            acc = acc + _dot(p_l.astype(BF16), vl_ref[0])
        o_ref[0, :, cols] = (acc / denom).astype(o_ref.dtype)


def _attention(p_q, p_ctx, p_lat, q_norm, k_norm, cos_t, sin_t):
    b, t, _ = p_q.shape
    n_c = p_ctx.shape[1]
    has_lat = p_lat is not None
    tq = min(t, 256)
    gw = ATT_GROUP * HEAD_DIM
    q_blk, k_blk, v_blk = 2048 // gw, (2048 + 1024) // HEAD_DIM, (2048 + 1024 + 256) // HEAD_DIM
    vec = pl.BlockSpec((1, HEAD_DIM), lambda bi, g, qi: (0, 0))
    in_specs = [pl.BlockSpec((1, tq, gw), lambda bi, g, qi: (bi, qi, q_blk + g)),
                pl.BlockSpec((1, n_c, HEAD_DIM), lambda bi, g, qi: (bi, 0, k_blk + g)),
                pl.BlockSpec((1, n_c, HEAD_DIM), lambda bi, g, qi: (bi, 0, v_blk + g))]
    args = [p_q, p_ctx, p_ctx]
    scratch = [pltpu.VMEM((n_c, HEAD_DIM), BF16)]
    if has_lat:
        n_l = p_lat.shape[1]
        in_specs += [pl.BlockSpec((1, n_l, HEAD_DIM), lambda bi, g, qi: (bi, 0, k_blk + g)),
                     pl.BlockSpec((1, n_l, HEAD_DIM), lambda bi, g, qi: (bi, 0, v_blk + g))]
        args += [p_lat, p_lat]
        scratch.append(pltpu.VMEM((n_l, HEAD_DIM), BF16))
    in_specs += [vec, vec]
    args += [q_norm.reshape(1, HEAD_DIM), k_norm.reshape(1, HEAD_DIM)]
    if has_lat:
        in_specs += [pl.BlockSpec((tq, HEAD_DIM), lambda bi, g, qi: (qi, 0)),
                     pl.BlockSpec((tq, HEAD_DIM), lambda bi, g, qi: (qi, 0)),
                     pl.BlockSpec((n_l, HEAD_DIM), lambda bi, g, qi: (0, 0)),
                     pl.BlockSpec((n_l, HEAD_DIM), lambda bi, g, qi: (0, 0))]
        args += [cos_t, sin_t, cos_t, sin_t]
    return pl.pallas_call(
        functools.partial(_attn_kernel, has_lat=has_lat),
        grid=(b, ATT_HEADS // ATT_GROUP, t // tq),
        in_specs=in_specs,
        out_specs=pl.BlockSpec((1, tq, gw), lambda bi, g, qi: (bi, qi, g)),
        out_shape=jax.ShapeDtypeStruct((b, t, ATT_HEADS * HEAD_DIM), BF16),
        scratch_shapes=scratch,
        compiler_params=_cparams("parallel", "parallel", "arbitrary"),
        name="attention_lat" if has_lat else "attention_ctx",
    )(*args)


def _conformer_kernel(x_ref, xp_ref, xn_ref, w_ref, b_ref, g_ref, be_ref, o_ref, xs, *, tt):
    i, n = pl.program_id(1), pl.num_programs(1)

    def glu(ref):
        v = ref[0].astype(F32)
        return v[:, :CONV_WIDTH] * jax.nn.sigmoid(v[:, CONV_WIDTH:])

    xs[0:HALO] = jnp.where(i > 0, glu(xp_ref), 0.0)
    xs[HALO:HALO + tt] = glu(x_ref)
    xs[HALO + tt:] = jnp.where(i < n - 1, glu(xn_ref), 0.0)
    pad = CONV_KERNEL // 2
    offs = [HALO - pad + k for k in range(CONV_KERNEL)]
    window = xs[...]
    rows = tt + 2 * HALO
    acc = None
    for phase in range(SUBLANES):
        group = [o for o in offs if o % SUBLANES == phase]
        if not group:
            continue
        base = window if phase == 0 else pltpu.roll(window, rows - phase, 0)
        for o in group:
            k = o - (HALO - pad)
            term = w_ref[k:k + 1, :] * base[o - phase:o - phase + tt, :]
            acc = term if acc is None else acc + term
    h = acc + b_ref[...]
    hc = h - jnp.mean(h, axis=-1, keepdims=True)
    var = jnp.mean(hc * hc, axis=-1, keepdims=True)
    y = hc * lax.rsqrt(var + NORM_EPS) * g_ref[...] + be_ref[...]
    o_ref[0] = _silu(y).astype(o_ref.dtype)


def _conformer(p, conv_w, conv_b, norm_g, norm_b):
    b, t, _ = p.shape
    tt = min(t, 256)
    nh = t // HALO
    cw = CONV_WIDTH
    vec = pl.BlockSpec((1, cw), lambda bi, i: (0, 0))
    return pl.pallas_call(
        functools.partial(_conformer_kernel, tt=tt),
        grid=(b, t // tt),
        in_specs=[pl.BlockSpec((1, tt, 2 * cw), lambda bi, i: (bi, i, 0)),
                  pl.BlockSpec((1, HALO, 2 * cw), lambda bi, i: (bi, jnp.maximum(i * (tt // HALO) - 1, 0), 0)),
                  pl.BlockSpec((1, HALO, 2 * cw), lambda bi, i: (bi, jnp.minimum((i + 1) * (tt // HALO), nh - 1), 0)),
                  pl.BlockSpec((CONV_KERNEL, cw), lambda bi, i: (0, 0)),
                  vec, vec, vec],
        out_specs=pl.BlockSpec((1, tt, cw), lambda bi, i: (bi, i, 0)),
        out_shape=jax.ShapeDtypeStruct((b, t, cw), BF16),
        scratch_shapes=[pltpu.VMEM((tt + 2 * HALO, cw), F32)],
        compiler_params=_cparams("parallel", "parallel"),
        name="conformer",
    )(p, p, p, conv_w, conv_b.reshape(1, cw), norm_g.reshape(1, cw), norm_b.reshape(1, cw))


def _head_ones(n, width):
    r = lax.broadcasted_iota(jnp.int32, (n, n), 0) // width
    c = lax.broadcasted_iota(jnp.int32, (n, n), 1) // width
    return jnp.where(r == c, 1.0, 0.0).astype(BF16)


def _softplus(x):
    return jnp.maximum(x, 0.0) + jnp.log(1.0 + jnp.exp(-jnp.abs(x)))


def _rwkv_feat_kernel(z_ref, zp_ref, zn_ref, mu_ref, w0_ref, w2_ref, a0_ref, a2_ref, g2_ref, kk_ref, ka_ref,
                      r_o, v_o, kk_o, lw0_o, lw1_o, kd0_o, kd1_o, b0_o, b1_o, gate_o, zs, *, tt):
    i, n = pl.program_id(1), pl.num_programs(1)
    zs[0:HALO] = jnp.where(i > 0, zp_ref[0].astype(F32), 0.0)
    zs[HALO:HALO + tt] = z_ref[0].astype(F32)
    zs[HALO + tt:] = jnp.where(i < n - 1, zn_ref[0].astype(F32), 0.0)
    window = zs[...]
    rows = tt + 2 * HALO
    z = window[HALO:HALO + tt]
    z_prev = pltpu.roll(window, 1, 0)[HALO:HALO + tt]
    z_next = pltpu.roll(window, rows - 1, 0)[HALO:HALO + tt]
    x = z + mu_ref[0:1, :] * (z_prev - z) + mu_ref[1:2, :] * (z_next - z)
    w = RWKV_WIDTH
    r, k, v = x[:, :w], x[:, w:2 * w], x[:, 2 * w:3 * w]
    lora = x[:, 3 * w:3 * w + LORA_IN]
    gate_in = x[:, 3 * w + LORA_IN:]
    r_o[0] = r.astype(r_o.dtype)
    v_o[0] = v.astype(v_o.dtype)
    kk = k * kk_ref[...]
    ones = _head_ones(LANES, RWKV_HEAD)
    sq = kk * kk
    ssum = jnp.concatenate([_split_dot(sq[:, c * LANES:(c + 1) * LANES], ones, 2) for c in range(w // LANES)], axis=-1)
    kk = kk * lax.rsqrt(ssum + 1e-12)
    kk_o[0] = kk.astype(kk_o.dtype)
    lora_t = jnp.tanh(lora).astype(BF16)
    lora_b = lora.astype(BF16)
    for d, (lw_o, kd_o, b_o) in enumerate(((lw0_o, kd0_o, b0_o), (lw1_o, kd1_o, b1_o))):
        wl = -_softplus(-(w0_ref[d:d + 1, :] + _dot(lora_t, w2_ref[d]))) - 0.5
        lw_o[0] = -jnp.exp(wl)
        a = jax.nn.sigmoid(a0_ref[d:d + 1, :] + _dot(lora_b, a2_ref[d]))
        kd_o[0] = (k * (1.0 + (a - 1.0) * ka_ref[...])).astype(kd_o.dtype)
        b_o[0] = (kk * a).astype(b_o.dtype)
    gate_o[0] = _dot(jax.nn.sigmoid(gate_in).astype(BF16), g2_ref[...]).astype(gate_o.dtype)


def _rwkv_features(z, mu, w0, w2p, a0, a2p, g2p, k_k, k_a):
    b, t, zw = z.shape
    tt = min(t, 256)
    nh = t // HALO
    w = RWKV_WIDTH
    vec = pl.BlockSpec((1, w), lambda bi, i: (0, 0))
    out_spec = pl.BlockSpec((1, tt, w), lambda bi, i: (bi, i, 0))
    outs = [jax.ShapeDtypeStruct((b, t, w), F32 if k in (3, 4) else BF16) for k in range(10)]
    return pl.pallas_call(
        functools.partial(_rwkv_feat_kernel, tt=tt),
        grid=(b, t // tt),
        in_specs=[pl.BlockSpec((1, tt, zw), lambda bi, i: (bi, i, 0)),
                  pl.BlockSpec((1, HALO, zw), lambda bi, i: (bi, jnp.maximum(i * (tt // HALO) - 1, 0), 0)),
                  pl.BlockSpec((1, HALO, zw), lambda bi, i: (bi, jnp.minimum((i + 1) * (tt // HALO), nh - 1), 0)),
                  pl.BlockSpec((2, zw), lambda bi, i: (0, 0)),
                  pl.BlockSpec((2, w), lambda bi, i: (0, 0)),
                  pl.BlockSpec((2, LORA_IN, w), lambda bi, i: (0, 0, 0)),
                  pl.BlockSpec((2, w), lambda bi, i: (0, 0)),
                  pl.BlockSpec((2, LORA_IN, w), lambda bi, i: (0, 0, 0)),
                  pl.BlockSpec((GATE_PAD, w), lambda bi, i: (0, 0)),
                  vec, vec],
        out_specs=[out_spec] * 10,
        out_shape=outs,
        scratch_shapes=[pltpu.VMEM((tt + 2 * HALO, zw), F32)],
        compiler_params=_cparams("parallel", "parallel"),
        name="rwkv_features",
    )(z, z, z, mu, w0, w2p, a0, a2p, g2p, k_k.reshape(1, w), k_a.reshape(1, w))


def _rwkv_chunk_terms(chains, want_y, between=()):
    pending = list(between)

    def tick():
        if pending:
            pending.pop(0)()

    n = len(chains)
    rng = range(n)
    c = chains[0][1].shape[0]
    low_lane = chains[0][6][3]
    block_mask = chains[0][6][4]
    r, v, kk, lw, kd, beta = ([ch[k] for ch in chains] for k in range(6))
    tri_incl, incl2, strict2 = ([ch[6][k] for ch in chains] for k in range(3))

    def stack_heads(x):
        return jnp.concatenate([jnp.where(low_lane, x, 0.0), jnp.where(low_lane, 0.0, x)], axis=0)

    def unstack(x):
        return jnp.where(low_lane, x[:c], x[c:])

    cs = [_ones_dot(tri_incl[i], lw[i], 3) for i in rng]
    tick()
    tot = [jnp.sum(lw[i], axis=0, keepdims=True) for i in rng]
    e_neg = [jnp.exp(-cs[i]) for i in rng]
    e_rem = [jnp.exp(tot[i] - cs[i]) for i in rng]
    kt_st = [stack_heads(kk[i] * jnp.exp(cs[i] - lw[i])) for i in rng]
    kt_sb = [kt_st[i].astype(BF16) for i in rng]
    bh_sb = [stack_heads(beta[i] * e_neg[i]).astype(BF16) for i in rng]
    kh_sb = [stack_heads(kd[i] * e_neg[i]).astype(BF16) for i in rng]
    right = [jnp.concatenate([beta[i] * e_rem[i], kd[i] * e_rem[i]], axis=0).astype(BF16) for i in rng]
    v_b = [v[i].astype(BF16) for i in rng]
    v2 = [jnp.concatenate([v_b[i], v_b[i]], axis=0) for i in rng]
    a_kb = [_dot_nt(kt_sb[i], bh_sb[i]) * strict2[i] for i in rng]
    a_kk = [(_dot_nt(kt_sb[i], kh_sb[i]) * strict2[i]).astype(BF16) for i in rng]
    tick()
    if want_y:
        rt = [r[i] * jnp.exp(cs[i]) for i in rng]
        rt_sb = [stack_heads(rt[i]).astype(BF16) for i in rng]
        a_rb = [(_dot_nt(rt_sb[i], bh_sb[i]) * incl2[i]).astype(BF16) for i in rng]
        a_rk = [(_dot_nt(rt_sb[i], kh_sb[i]) * incl2[i]).astype(BF16) for i in rng]
        tick()
    rhs0 = [_dot(a_kk[i], v2[i]) for i in rng]
    tick()
    p = [-a_kb[i] for i in rng]
    x = list(p)
    for _ in range(int(np.log2(c)) - 1):
        p_b = [p[i].astype(BF16) for i in rng]
        x_b = [x[i].astype(BF16) for i in rng]
        p = [_dot(p_b[i], p_b[i]) for i in rng]
        xp = [_dot(x_b[i], p[i].astype(BF16)) for i in rng]
        x = [x[i] + p[i] + xp[i] for i in rng]
        tick()
    both = [_dot(x[i].astype(BF16), jnp.concatenate([rhs0[i].astype(BF16), kt_sb[i]], axis=1)) for i in rng]
    u0 = [unstack(-(rhs0[i] + both[i][:, :LANES])) for i in rng]
    kq_st = [kt_st[i] + both[i][:, LANES:] for i in rng]
    kq = [unstack(kq_st[i]) for i in rng]
    lhs_t = [jnp.concatenate([jnp.concatenate([u0[i], v[i]], axis=0).T,
                              jnp.concatenate([kq[i], jnp.zeros_like(kq[i])], axis=0).T], axis=0).astype(BF16)
             for i in rng]
    tick()
    gq = [_dot(lhs_t[i], right[i]) for i in rng]
    tick()
    if want_y:
        u0_b = [u0[i].astype(BF16) for i in rng]
        y0 = [_dot(jnp.concatenate([a_rb[i], a_rk[i]], axis=1), jnp.concatenate([u0_b[i], u0_b[i], v2[i]], axis=0))
              for i in rng]
        rq = [_dot(a_rb[i], kq_st[i].astype(BF16)) for i in rng]
    while pending:
        tick()
    out = []
    for i in rng:
        g_add = gq[i][:LANES] * block_mask
        q_mat = (gq[i][LANES:] * block_mask).astype(BF16)
        if want_y:
            out.append((jnp.exp(tot[i]), q_mat, g_add, (rt[i] - unstack(rq[i])).astype(BF16), unstack(y0[i])))
        else:
            out.append((jnp.exp(tot[i]), q_mat, g_add, None, None))
    return out


def _rwkv_consts(c, reverse):
    t = lax.broadcasted_iota(jnp.int32, (c, c), 0)
    s = lax.broadcasted_iota(jnp.int32, (c, c), 1)
    incl = (s >= t) if reverse else (s <= t)
    tri_incl = jnp.where(incl, 1.0, 0.0).astype(BF16)
    t2 = lax.broadcasted_iota(jnp.int32, (2 * c, 2 * c), 0)
    s2 = lax.broadcasted_iota(jnp.int32, (2 * c, 2 * c), 1)
    same = (t2 // c) == (s2 // c)
    order = (s2 >= t2) if reverse else (s2 <= t2)
    incl2 = jnp.where(same & order, 1.0, 0.0)
    strict2 = jnp.where(same & order & (s2 != t2), 1.0, 0.0)
    low_lane = lax.broadcasted_iota(jnp.int32, (1, LANES), 1) < RWKV_HEAD
    bi = lax.broadcasted_iota(jnp.int32, (LANES, LANES), 0) // RWKV_HEAD
    bj = lax.broadcasted_iota(jnp.int32, (LANES, LANES), 1) // RWKV_HEAD
    block_mask = jnp.where(bi == bj, 1.0, 0.0)
    return tri_incl, incl2, strict2, low_lane, block_mask


def _rwkv_kernel(r_l, v_l, kk_l, lw0_l, lw1_l, kd0_l, kd1_l, b0_l, b1_l, gate_l,
                 v_c, kk_c, lw0_c, lw1_c, kd0_c, kd1_c, b0_c, b1_c,
                 rk_ref, lng_ref, lnb_ref, o_ref, acc_s, gam_s, q_s, g_s, re_s):
    c = RWKV_CHUNK
    n_l, n_c = r_l.shape[1] // c, v_c.shape[1] // c
    consts = (_rwkv_consts(c, False), _rwkv_consts(c, True))
    lat_dirs = ((lw0_l, kd0_l, b0_l), (lw1_l, kd1_l, b1_l))
    ctx_dirs = ((lw0_c, kd0_c, b0_c), (lw1_c, kd1_c, b1_c))
    acc_s[...] = jnp.zeros_like(acc_s)
    state = [jnp.zeros((LANES, LANES), F32), jnp.zeros((LANES, LANES), F32)]

    def chunk_rows(ci):
        return slice(ci * c, (ci + 1) * c)

    def scan_step(slot_f, slot_b, lat_f=None, lat_b=None):
        def step():
            nxt = []
            for d, slot, lat in ((0, slot_f, lat_f), (1, slot_b, lat_b)):
                s_in = state[d]
                s_b = s_in.astype(BF16)
                nxt.append(s_in * gam_s[d, slot] - _dot(s_b, q_s[d, slot]) + g_s[d, slot])
                if lat is not None:
                    acc_s[chunk_rows(lat), :] += _dot_nt(re_s[d, lat], s_b)
            state[:] = nxt
        return step

    def build_terms(groups, refs, dirs, slot0, want_y, between):
        r_ref, v_ref, kk_ref = refs
        chains, where = [], []
        for d, (lw_r, kd_r, b_r) in enumerate(dirs):
            for ci in groups[d]:
                rows = chunk_rows(ci)
                r = r_ref[0, rows, :].astype(F32) if want_y else None
                chains.append((r, v_ref[0, rows, :].astype(F32), kk_ref[0, rows, :].astype(F32), lw_r[0, rows, :],
                               kd_r[0, rows, :].astype(F32), b_r[0, rows, :].astype(F32), consts[d]))
                where.append((d, ci))
        terms = _rwkv_chunk_terms(chains, want_y, between)
        for (d, ci), (gam, q_mat, g_add, r_eff, y0) in zip(where, terms):
            gam_s[d, slot0 + ci] = gam
            q_s[d, slot0 + ci] = q_mat
            g_s[d, slot0 + ci] = g_add
            if want_y:
                re_s[d, ci] = r_eff
                acc_s[chunk_rows(ci), :] += y0

    group = min(RWKV_GROUP, n_l)
    n_groups = n_l // group
    build_terms((range(n_c), range(n_c)), (None, v_c, kk_c), ctx_dirs, 0, False, ())
    steps = [scan_step(k, n_c - 1 - k) for k in range(n_c)]
    for gi in range(n_groups):
        gb = n_groups - 1 - gi
        build_terms((range(gi * group, (gi + 1) * group), range(gb * group, (gb + 1) * group)),
                    (r_l, v_l, kk_l), lat_dirs, n_c, True, steps)
        steps = [scan_step(n_c + gi * group + k, n_c + (gb + 1) * group - 1 - k,
                           gi * group + k, (gb + 1) * group - 1 - k) for k in range(group)]
    for step in steps:
        step()

    ones = _head_ones(LANES, RWKV_HEAD)
    wkv = acc_s[...]
    inv_n = 1.0 / RWKV_HEAD
    mean = _split_dot(wkv, ones, 2) * inv_n
    xc = wkv - mean
    var = _split_dot(xc * xc, ones, 2) * inv_n
    y = xc * lax.rsqrt(var + RWKV_GN_EPS) * lng_ref[...] + lnb_ref[...]
    r = r_l[0].astype(F32)
    rk = rk_ref[...]
    bonus = (_split_dot(r * kd0_l[0].astype(F32) * rk, ones, 2)
             + _split_dot(r * kd1_l[0].astype(F32) * rk, ones, 2)) * v_l[0].astype(F32)
    o_ref[0] = ((y + bonus) * gate_l[0].astype(F32)).astype(o_ref.dtype)


def _rwkv_mix(feat_l, feat_c, r_k, ln_g, ln_b):
    r_l, v_l, kk_l, lw0_l, lw1_l, kd0_l, kd1_l, b0_l, b1_l, gate_l = feat_l
    _, v_c, kk_c, lw0_c, lw1_c, kd0_c, kd1_c, b0_c, b1_c, _ = feat_c
    b, t, w = r_l.shape
    t_c = v_c.shape[1]
    n_slots = (t_c + t) // RWKV_CHUNK
    lat = pl.BlockSpec((1, t, LANES), lambda bi, p: (bi, 0, p))
    ctx = pl.BlockSpec((1, t_c, LANES), lambda bi, p: (bi, 0, p))
    vec = pl.BlockSpec((1, LANES), lambda bi, p: (0, p))
    return pl.pallas_call(
        _rwkv_kernel,
        grid=(b, w // LANES),
        in_specs=[lat] * 10 + [ctx] * 8 + [vec] * 3,
        out_specs=lat,
        out_shape=jax.ShapeDtypeStruct((b, t, w), BF16),
        scratch_shapes=[pltpu.VMEM((t, LANES), F32),
                        pltpu.VMEM((2, n_slots, 1, LANES), F32),
                        pltpu.VMEM((2, n_slots, LANES, LANES), BF16),
                        pltpu.VMEM((2, n_slots, LANES, LANES), F32),
                        pltpu.VMEM((2, t // RWKV_CHUNK, RWKV_CHUNK, LANES), BF16)],
        compiler_params=_cparams("parallel", "parallel"),
        name="rwkv_mix",
    )(r_l, v_l, kk_l, lw0_l, lw1_l, kd0_l, kd1_l, b0_l, b1_l, gate_l,
      v_c, kk_c, lw0_c, lw1_c, kd0_c, kd1_c, b0_c, b1_c,
      r_k.reshape(1, w), ln_g.reshape(1, w), ln_b.reshape(1, w))


def _retention_kernel(dl_ref, q_l, k_l, v_l, rg_l, k_c, v_c, cos_ref, sin_ref, gg_ref, gb_ref, o_ref,
                      q_s, k_s, kv_s, rin_s):
    c = RET_CHUNK
    n_l, n_c = q_l.shape[1] // c, k_c.shape[1] // c
    h = pl.program_id(1)
    k_scale = HEAD_DIM ** -0.5
    q_s[...] = _rope(q_l[0].astype(F32), cos_ref[...], sin_ref[...]).astype(BF16)
    k_s[...] = (_rope(k_l[0].astype(F32), cos_ref[...], sin_ref[...]) * k_scale).astype(BF16)
    ti = lax.broadcasted_iota(jnp.int32, (c, c), 0).astype(F32)
    si = lax.broadcasted_iota(jnp.int32, (c, c), 1).astype(F32)
    col = lax.broadcasted_iota(jnp.int32, (c, 1), 0).astype(F32)
    row = lax.broadcasted_iota(jnp.int32, (1, c), 1).astype(F32)
    lg = [jax.nn.log_sigmoid(jnp.full((1, LANES), dl_ref[d, h], F32))[:, :1] for d in range(2)]
    fwd = jnp.where(ti >= si, jnp.exp(lg[0] * jnp.maximum(ti - si, 0.0)), 0.0)
    bwd = jnp.where(si >= ti, jnp.exp(lg[1] * jnp.maximum(si - ti, 0.0)), 0.0)
    inner = fwd + bwd
    q_decay = (jnp.exp(lg[0] * (col + 1.0)), jnp.exp(lg[1] * (c - col)))
    k_decay = (jnp.exp(lg[0] * (c - 1.0 - row)), jnp.exp(lg[1] * row))
    chunk_decay = (jnp.exp(lg[0] * c), jnp.exp(lg[1] * c))

    def kv_body(ii, carry, *, group, slot0, lat):
        ks, vs, slots = [], [], []
        for g in range(group):
            ci = ii * group + g
            rows = pl.ds(pl.multiple_of(ci * c, c), c)
            ks.append(k_s[rows, :].astype(F32) if lat else k_c[0, rows, :].astype(F32) * k_scale)
            vs.append(v_l[0, rows, :] if lat else v_c[0, rows, :])
            slots.append(slot0 + ci)
        kts = [k.T for k in ks]
        lhs = [jnp.concatenate([kt * k_decay[0], kt * k_decay[1]], axis=0).astype(BF16) for kt in kts]
        kv = [_dot(a, v) for a, v in zip(lhs, vs)]
        for slot, x in zip(slots, kv):
            kv_s[0, slot] = x[:HEAD_DIM]
            kv_s[1, slot] = x[HEAD_DIM:]
        return carry

    g_c = min(RET_GROUP, n_c)
    lax.fori_loop(0, n_c // g_c, functools.partial(kv_body, group=g_c, slot0=0, lat=False), 0)
    lax.fori_loop(0, n_l // RET_GROUP, functools.partial(kv_body, group=RET_GROUP, slot0=n_c, lat=True), 0)

    def scan_body(k, carry, *, n, slot0, record):
        r_f, r_b = carry
        kb = n - 1 - k
        if record:
            rin_s[0, k] = r_f.astype(BF16)
            rin_s[1, kb] = r_b.astype(BF16)
        return r_f * chunk_decay[0] + kv_s[0, slot0 + k], r_b * chunk_decay[1] + kv_s[1, slot0 + kb]

    zero = jnp.zeros((HEAD_DIM, RET_V_DIM), F32)
    carry = lax.fori_loop(0, n_c, functools.partial(scan_body, n=n_c, slot0=0, record=False), (zero, zero))
    lax.fori_loop(0, n_l, functools.partial(scan_body, n=n_l, slot0=n_c, record=True), carry)

    def out_body(ii, carry):
        qs, kbs, vs, rows_l = [], [], [], []
        for g in range(RET_GROUP):
            ci = ii * RET_GROUP + g
            rows = pl.ds(pl.multiple_of(ci * c, c), c)
            rows_l.append((ci, rows))
            qs.append(q_s[rows, :])
            kbs.append(k_s[rows, :])
            vs.append(v_l[0, rows, :])
        s = [(_dot_nt(q, k) * inner).astype(BF16) for q, k in zip(qs, kbs)]
        qd = []
        for q in qs:
            qf = q.astype(F32)
            qd.append(jnp.concatenate([qf * q_decay[0], qf * q_decay[1]], axis=1).astype(BF16))
        r_in = [jnp.concatenate([rin_s[0, ci], rin_s[1, ci]], axis=0) for ci, _ in rows_l]
        o = [_dot(s[g], vs[g]) + _dot(qd[g], r_in[g]) for g in range(RET_GROUP)]
        for (ci, rows), x in zip(rows_l, o):
            xc = x - jnp.mean(x, axis=-1, keepdims=True)
            var = jnp.mean(xc * xc, axis=-1, keepdims=True)
            y = xc * lax.rsqrt(var + NORM_EPS) * gg_ref[...] + gb_ref[...]
            o_ref[0, rows, :] = (y * _silu(rg_l[0, rows, :].astype(F32))).astype(o_ref.dtype)
        return carry

    lax.fori_loop(0, n_l // RET_GROUP, out_body, 0)


def _retention(p_l, p_c, decay_logit, gn_g, gn_b, cos_t, sin_t):
    b, t, _ = p_l.shape
    t_c = p_c.shape[1]
    nh, dk, dv = RET_HEADS, HEAD_DIM, RET_V_DIM
    k_blk, v_blk, g_blk = nh, (2 * nh * dk) // dv, (2 * nh * dk + nh * dv) // dv
    return pl.pallas_call(
        _retention_kernel,
        grid=(b, nh),
        in_specs=[pl.BlockSpec(memory_space=pltpu.SMEM),
                  pl.BlockSpec((1, t, dk), lambda bi, h: (bi, 0, h)),
                  pl.BlockSpec((1, t, dk), lambda bi, h: (bi, 0, k_blk + h)),
                  pl.BlockSpec((1, t, dv), lambda bi, h: (bi, 0, v_blk + h)),
                  pl.BlockSpec((1, t, dv), lambda bi, h: (bi, 0, g_blk + h)),
                  pl.BlockSpec((1, t_c, dk), lambda bi, h: (bi, 0, k_blk + h)),
                  pl.BlockSpec((1, t_c, dv), lambda bi, h: (bi, 0, v_blk + h)),
                  pl.BlockSpec((t, dk), lambda bi, h: (0, 0)),
                  pl.BlockSpec((t, dk), lambda bi, h: (0, 0)),
                  pl.BlockSpec((1, dv), lambda bi, h: (0, h)),
                  pl.BlockSpec((1, dv), lambda bi, h: (0, h))],
        out_specs=pl.BlockSpec((1, t, dv), lambda bi, h: (bi, 0, h)),
        out_shape=jax.ShapeDtypeStruct((b, t, nh * dv), BF16),
        scratch_shapes=[pltpu.VMEM((t, dk), BF16), pltpu.VMEM((t, dk), BF16),
                        pltpu.VMEM((2, (t_c + t) // RET_CHUNK, dk, dv), F32),
                        pltpu.VMEM((2, t // RET_CHUNK, dk, dv), BF16)],
        compiler_params=_cparams("parallel", "parallel"),
        name="retention",
    )(decay_logit, p_l, p_l, p_l, p_l, p_c, p_c, cos_t, sin_t, gn_g.reshape(1, nh * dv), gn_b.reshape(1, nh * dv))


def kernel(x, c, ctx, c_ctx, mod_w, mod_b, norm_mix_g, norm_ffn_g, ffn_w_up, ffn_conv_w, ffn_conv_b, ffn_w_down, ab_w_in, ab_q_norm, ab_k_norm, ab_conv_w, ab_conv_b, ab_conv_norm_g, ab_conv_norm_b, ab_w_out, cd_w_in, cd_shift_mu, rwkv_w0, rwkv_w2, rwkv_a0, rwkv_a2, rwkv_g2, rwkv_k_k, rwkv_k_a, rwkv_r_k, rwkv_ln_g, rwkv_ln_b, ret_decay_logit, ret_gn_g, ret_gn_b, cd_w_out, final_norm_g):
    b, t, d = x.shape
    t_c = ctx.shape[1]
    att_w = ATT_HEADS * HEAD_DIM
    kv_w = att_w // ATT_GROUP

    pad_rows = (-(b + 1)) % 8
    c_all = jnp.concatenate([c, c_ctx[None, :], jnp.zeros((pad_rows, d), F32)], axis=0)
    mod = _modulation(c_all, mod_w, mod_b)

    def mod_vectors(layer):
        lat = [m.reshape(b, 1, d) for m in jnp.split(mod[layer, :b], 6, axis=-1)]
        cx = [m.reshape(1, 1, d) for m in jnp.split(mod[layer, b:b + 1], 6, axis=-1)]
        return lat, cx

    cos_t, sin_t = _rope_tables(t)
    h = x
    hc = ctx.reshape(1, b * t_c, d)

    (sh1, sc1, g1, sh2, sc2, g2), (csh1, csc1, cg1, csh2, csc2, cg2) = mod_vectors(0)
    w_in = ab_w_in[0]
    glu0 = att_w + 2 * kv_w
    w_in = jnp.concatenate([w_in[:, glu0:], w_in[:, :glu0]], axis=1).astype(BF16)
    p_l = _norm_mod_matmul(h, norm_mix_g[0], sh1, sc1, w_in, BF16, "ab_in_lat")
    p_c = _norm_mod_matmul(hc, norm_mix_g[0], csh1, csc1, w_in, BF16, "ab_in_ctx").reshape(b, t_c, -1)
    att_l = _attention(p_l, p_c, p_l, ab_q_norm[0], ab_k_norm[0], cos_t, sin_t)
    att_c = _attention(p_c, p_c, None, ab_q_norm[0], ab_k_norm[0], None, None)
    cv_l = _conformer(p_l, ab_conv_w[0], ab_conv_b[0], ab_conv_norm_g[0], ab_conv_norm_b[0])
    cv_c = _conformer(p_c, ab_conv_w[0], ab_conv_b[0], ab_conv_norm_g[0], ab_conv_norm_b[0])
    w_out = ab_w_out[0].astype(BF16)
    w_o = [w_out[:att_w], w_out[att_w:]]
    h = _matmul_gated_residual([att_l, cv_l], w_o, h, g1, "ab_out_lat")
    hc = _matmul_gated_residual([att_c.reshape(1, b * t_c, -1), cv_c.reshape(1, b * t_c, -1)], w_o, hc, cg1,
                                "ab_out_ctx")
    w_up, w_down = ffn_w_up.astype(BF16), ffn_w_down.astype(BF16)
    h = _conv_ffn(h, norm_ffn_g[0], sh2, sc2, g2, w_up, ffn_conv_w[0], ffn_conv_b[0], w_down, 0, final_norm_g,
                  t, False, "ffn0_lat")
    hc = _conv_ffn(hc, norm_ffn_g[0], csh2, csc2, cg2, w_up, ffn_conv_w[0], ffn_conv_b[0], w_down, 0,
                   final_norm_g, t_c, False, "ffn0_ctx")

    (sh1, sc1, g1, sh2, sc2, g2), (csh1, csc1, _, _, _, _) = mod_vectors(1)
    rw_in = 3 * RWKV_WIDTH + 2 * RWKV_HEAD + GATE_LORA
    zpad = RWKV_IN_PAD - rw_in
    w_z = jnp.pad(cd_w_in[0][:, :rw_in], ((0, 0), (0, zpad))).astype(BF16)
    w_ret = cd_w_in[0][:, rw_in:].astype(BF16)
    z_l = _norm_mod_matmul(h, norm_mix_g[1], sh1, sc1, w_z, BF16, "cd_in_z_lat")
    z_c = _norm_mod_matmul(hc, norm_mix_g[1], csh1, csc1, w_z, BF16, "cd_in_z_ctx").reshape(b, t_c, -1)
    rp_l = _norm_mod_matmul(h, norm_mix_g[1], sh1, sc1, w_ret, BF16, "cd_in_ret_lat")
    rp_c = _norm_mod_matmul(hc, norm_mix_g[1], csh1, csc1, w_ret, BF16, "cd_in_ret_ctx").reshape(b, t_c, -1)

    mu = jnp.pad(cd_shift_mu[0], ((0, 0), (0, zpad)))
    lora_pad = LORA_IN - RWKV_HEAD
    w2p = jnp.pad(rwkv_w2[0], ((0, 0), (0, lora_pad), (0, 0))).astype(BF16)
    a2p = jnp.pad(rwkv_a2[0], ((0, 0), (lora_pad, 0), (0, 0))).astype(BF16)
    g2p = jnp.pad(rwkv_g2[0], ((0, GATE_PAD - GATE_LORA), (0, 0))).astype(BF16)
    feat_l = _rwkv_features(z_l, mu, rwkv_w0[0], w2p, rwkv_a0[0], a2p, g2p, rwkv_k_k[0], rwkv_k_a[0])
    feat_c = _rwkv_features(z_c, mu, rwkv_w0[0], w2p, rwkv_a0[0], a2p, g2p, rwkv_k_k[0], rwkv_k_a[0])
    y_c = _rwkv_mix(feat_l, feat_c, rwkv_r_k[0], rwkv_ln_g[0], rwkv_ln_b[0])
    y_d = _retention(rp_l, rp_c, ret_decay_logit[0], ret_gn_g[0], ret_gn_b[0], cos_t, sin_t)
    w_out = cd_w_out[0].astype(BF16)
    h = _matmul_gated_residual([y_c, y_d], [w_out[:RWKV_WIDTH], w_out[RWKV_WIDTH:]], h, g1, "cd_out_lat")
    return _conv_ffn(h, norm_ffn_g[1], sh2, sc2, g2, w_up, ffn_conv_w[1], ffn_conv_b[1], w_down, 1, final_norm_g,
                     t, True, "ffn1_lat")
```

```python
import functools

import jax
import jax.numpy as jnp
import numpy as np
from jax import lax
from jax.experimental import pallas as pl
from jax.experimental.pallas import tpu as pltpu

F32 = jnp.float32
BF16 = jnp.bfloat16

NORM_EPS = 1e-6
GRID_W = 64
ROPE_BASE = 10000.0
HEAD_DIM = 128
ATT_HEADS = 8
ATT_GROUP = 4
CONV_WIDTH = 1024
CONV_KERNEL = 31
RWKV_WIDTH = 1024
RWKV_HEAD = 64
RWKV_GN_EPS = 64e-5
RWKV_CHUNK = 64
RWKV_GROUP = 8
LORA_IN = 128
GATE_LORA = 160
GATE_PAD = 384
RWKV_IN_PAD = 3584
RET_HEADS = 8
RET_V_DIM = 256
RET_CHUNK = 128
RET_GROUP = 4
FFN_DIM = 5632
HALO = 16
NORM_PIECE = 16
FFN_ROW_TILE = 1024
FFN_ACC_COLS = 512
FFN_OUT_ROWS = 64
INPROJ_COL_TILES = (1792, 1024, 512)
OUTPROJ_COL_TILE = 1024
INPROJ_ROW_CHUNK = 256
LANES = 128
SUBLANES = 8
LOG2_E = 1.4426950408889634

VMEM_LIMIT_BYTES = 52 * 1024 * 1024


def _cparams(*sem):
    return pltpu.CompilerParams(dimension_semantics=sem, vmem_limit_bytes=VMEM_LIMIT_BYTES)


def _dot(a, b):
    return jnp.dot(a, b, preferred_element_type=F32)


def _dot_nt(a, b):
    return lax.dot_general(a, b, (((1,), (1,)), ((), ())), preferred_element_type=F32)


def _split_dot(x, ones_bf16, terms):
    acc = None
    rem = x
    for _ in range(terms):
        piece = rem.astype(BF16)
        part = _dot(piece, ones_bf16)
        acc = part if acc is None else acc + part
        rem = rem - piece.astype(F32)
    return acc


def _ones_dot(ones_bf16, x, terms):
    acc = None
    rem = x
    for _ in range(terms):
        piece = rem.astype(BF16)
        part = _dot(ones_bf16, piece)
        acc = part if acc is None else acc + part
        rem = rem - piece.astype(F32)
    return acc


def _rms(x, g):
    return x * lax.rsqrt(jnp.mean(x * x, axis=-1, keepdims=True) + NORM_EPS) * g


def _silu(x):
    return x * jax.nn.sigmoid(x)


def _rope(x, cos, sin_signed):
    n = x.shape[-1]
    lane = lax.broadcasted_iota(jnp.int32, x.shape, x.ndim - 1)
    partner = jnp.where((lane % 64) < 32, pltpu.roll(x, n - 32, x.ndim - 1), pltpu.roll(x, 32, x.ndim - 1))
    return x * cos + partner * sin_signed


def _rope_tables(n_tok):
    rows = n_tok // GRID_W
    row = jnp.repeat(jnp.arange(rows, dtype=F32), GRID_W)
    col = jnp.tile(jnp.arange(GRID_W, dtype=F32), rows)
    axis_dim = HEAD_DIM // 2
    inv_freq = ROPE_BASE ** (-jnp.arange(0, axis_dim, 2, dtype=F32) / axis_dim)
    ang = jnp.concatenate([row[:, None] * inv_freq, col[:, None] * inv_freq], axis=-1)
    cos, sin = jnp.cos(ang), jnp.sin(ang)
    c_r, c_c, s_r, s_c = cos[:, :32], cos[:, 32:], sin[:, :32], sin[:, 32:]
    cos_t = jnp.concatenate([c_r, c_r, c_c, c_c], axis=-1)
    sin_t = jnp.concatenate([-s_r, s_r, -s_c, s_c], axis=-1)
    return cos_t, sin_t


def _mod_kernel(c_ref, w_ref, b_ref, o_ref):
    s = _silu(c_ref[...]).astype(BF16)
    o_ref[0] = _dot(s, w_ref[0].astype(BF16)) + b_ref[0]


def _modulation(c_all, mod_w, mod_b):
    n_layers, d, n = mod_w.shape
    rows = c_all.shape[0]
    tn = 1024
    return pl.pallas_call(
        _mod_kernel,
        grid=(n_layers, n // tn),
        in_specs=[pl.BlockSpec((rows, d), lambda l, j: (0, 0)),
                  pl.BlockSpec((1, d, tn), lambda l, j: (l, 0, j)),
                  pl.BlockSpec((1, 1, tn), lambda l, j: (l, 0, j))],
        out_specs=pl.BlockSpec((1, rows, tn), lambda l, j: (l, 0, j)),
        out_shape=jax.ShapeDtypeStruct((n_layers, rows, n), F32),
        compiler_params=_cparams("parallel", "parallel"),
        name="modulation",
    )(c_all, mod_w, mod_b.reshape(n_layers, 1, n))


def _norm_mod_rows(src, dst, dst_row0, n_rows, gain, shift):
    for r0 in range(0, n_rows, NORM_PIECE):
        x = src(r0)
        inv = lax.rsqrt(jnp.mean(x * x, axis=-1, keepdims=True) + NORM_EPS)
        dst[dst_row0 + r0:dst_row0 + r0 + NORM_PIECE, :] = (x * inv * gain + shift).astype(dst.dtype)


def _inproj_kernel(h_ref, g_ref, sh_ref, sc_ref, w_ref, o_ref, u_s, *, tm):
    j = pl.program_id(2)

    @pl.when(j == 0)
    def _():
        gain = g_ref[...] * (1.0 + sc_ref[0])
        shift = sh_ref[0]
        chunk = min(tm, INPROJ_ROW_CHUNK)
        for c0 in range(0, tm, chunk):
            _norm_mod_rows(lambda r0: h_ref[0, c0 + r0:c0 + r0 + NORM_PIECE, :], u_s, c0, chunk, gain, shift)
            o_ref[0, c0:c0 + chunk, :] = _dot(u_s[c0:c0 + chunk, :], w_ref[...]).astype(o_ref.dtype)

    @pl.when(j > 0)
    def _():
        o_ref[0] = _dot(u_s[...], w_ref[...]).astype(o_ref.dtype)


def _norm_mod_matmul(h, g, shift, scale, w, out_dtype, name):
    bm, t, d = h.shape
    n = w.shape[1]
    tm = min(t, 1024)
    tn = next(c for c in INPROJ_COL_TILES if n % c == 0)
    return pl.pallas_call(
        functools.partial(_inproj_kernel, tm=tm),
        grid=(bm, t // tm, n // tn),
        in_specs=[pl.BlockSpec((1, tm, d), lambda b, i, j: (b, i, 0)),
                  pl.BlockSpec((1, d), lambda b, i, j: (0, 0)),
                  pl.BlockSpec((1, 1, d), lambda b, i, j: (b, 0, 0)),
                  pl.BlockSpec((1, 1, d), lambda b, i, j: (b, 0, 0)),
                  pl.BlockSpec((d, tn), lambda b, i, j: (0, j))],
        out_specs=pl.BlockSpec((1, tm, tn), lambda b, i, j: (b, i, j)),
        out_shape=jax.ShapeDtypeStruct((bm, t, n), out_dtype),
        scratch_shapes=[pltpu.VMEM((tm, d), BF16)],
        compiler_params=_cparams("parallel", "parallel", "arbitrary"),
        name=name,
    )(h, g.reshape(1, d), shift, scale, w)


def _outproj_kernel(*refs, n_lhs):
    lhs, ws = refs[:n_lhs], refs[n_lhs:2 * n_lhs]
    res_ref, gt_ref, o_ref = refs[2 * n_lhs:]
    acc = _dot(lhs[0][0], ws[0][...])
    for a, w in zip(lhs[1:], ws[1:]):
        acc = acc + _dot(a[0], w[...])
    o_ref[0] = res_ref[0] + gt_ref[0] * acc


def _matmul_gated_residual(lhs_list, w_list, res, gate, name):
    bm, t, d = res.shape
    tm = min(t, 1024)
    tn = OUTPROJ_COL_TILE
    in_specs = [pl.BlockSpec((1, tm, a.shape[-1]), lambda b, i, j: (b, i, 0)) for a in lhs_list]
    in_specs += [pl.BlockSpec((w.shape[0], tn), lambda b, i, j: (0, j)) for w in w_list]
    in_specs += [pl.BlockSpec((1, tm, tn), lambda b, i, j: (b, i, j)),
                 pl.BlockSpec((1, 1, tn), lambda b, i, j: (b, 0, j))]
    return pl.pallas_call(
        functools.partial(_outproj_kernel, n_lhs=len(lhs_list)),
        grid=(bm, t // tm, d // tn),
        in_specs=in_specs,
        out_specs=pl.BlockSpec((1, tm, tn), lambda b, i, j: (b, i, j)),
        out_shape=jax.ShapeDtypeStruct((bm, t, d), F32),
        compiler_params=_cparams("parallel", "parallel", "arbitrary"),
        name=name,
    )(*lhs_list, *w_list, res, gate)


def _ffn_kernel(h_ref, hp_ref, hn_ref, g_ref, sh_ref, sc_ref, gt_ref, wg_ref, wv_ref, cw_ref, cb_ref,
                wd_ref, fg_ref, o_ref, u_s, *, tm, seq_len, final_norm):
    i, j, nj = pl.program_id(1), pl.program_id(2), pl.num_programs(2)

    @pl.when(j == 0)
    def _():
        gain = g_ref[...] * (1.0 + sc_ref[0])
        shift = sh_ref[0]
        _norm_mod_rows(lambda r0: hp_ref[0, r0:r0 + NORM_PIECE, :], u_s, 0, HALO, gain, shift)
        _norm_mod_rows(lambda r0: h_ref[0, r0:r0 + NORM_PIECE, :], u_s, HALO, tm, gain, shift)
        _norm_mod_rows(lambda r0: hn_ref[0, r0:r0 + NORM_PIECE, :], u_s, HALO + tm, HALO, gain, shift)
        o_ref[0] = jnp.zeros(o_ref.shape[1:], F32)

    rows = tm + 2 * HALO
    gate_lin = _dot(u_s[...], wg_ref[...])
    val = _dot(u_s[HALO:HALO + tm], wv_ref[...])
    pos = (i * tm + lax.broadcasted_iota(jnp.int32, (tm, 1), 0)) % seq_len
    g_prev = jnp.where(pos != 0, pltpu.roll(gate_lin, 1, 0)[HALO:HALO + tm], 0.0)
    g_next = jnp.where(pos != seq_len - 1, pltpu.roll(gate_lin, rows - 1, 0)[HALO:HALO + tm], 0.0)
    g_cur = gate_lin[HALO:HALO + tm]
    cw = cw_ref[...]
    conv = cw[0:1] * g_prev + cw[1:2] * g_cur + cw[2:3] * g_next + cb_ref[...]
    act = (_silu(conv) * val).astype(BF16)
    d = o_ref.shape[2]
    for c0 in range(0, d, FFN_ACC_COLS):
        o_ref[0, :, c0:c0 + FFN_ACC_COLS] += _dot(act, wd_ref[:, c0:c0 + FFN_ACC_COLS])

    @pl.when(j == nj - 1)
    def _():
        for r0 in range(0, tm, FFN_OUT_ROWS):
            rs = slice(r0, r0 + FFN_OUT_ROWS)
            y = h_ref[0, rs, :] + gt_ref[0] * o_ref[0, rs, :]
            if final_norm:
                y = _rms(y, fg_ref[...])
            o_ref[0, rs, :] = y


def _conv_ffn(h, g, shift, scale, gate, w_up, conv_w, conv_b, w_down, layer, final_g, seq_len, final_norm, name):
    bm, t, d = h.shape
    f = w_down.shape[1]
    tm = min(t, FFN_ROW_TILE)
    tf = 512
    nh = t // HALO
    kern = functools.partial(_ffn_kernel, tm=tm, seq_len=seq_len, final_norm=final_norm)
    vec = pl.BlockSpec((1, 1, d), lambda b, i, j: (b, 0, 0))
    return pl.pallas_call(
        kern,
        grid=(bm, t // tm, f // tf),
        in_specs=[pl.BlockSpec((1, tm, d), lambda b, i, j: (b, i, 0), pipeline_mode=pl.Buffered(1)),
                  pl.BlockSpec((1, HALO, d), lambda b, i, j: (b, jnp.maximum(i * (tm // HALO) - 1, 0), 0)),
                  pl.BlockSpec((1, HALO, d), lambda b, i, j: (b, jnp.minimum((i + 1) * (tm // HALO), nh - 1), 0)),
                  pl.BlockSpec((1, d), lambda b, i, j: (0, 0)),
                  vec, vec, vec,
                  pl.BlockSpec((None, d, tf), lambda b, i, j: (layer, 0, j)),
                  pl.BlockSpec((None, d, tf), lambda b, i, j: (layer, 0, f // tf + j)),
                  pl.BlockSpec((3, tf), lambda b, i, j: (0, j)),
                  pl.BlockSpec((1, tf), lambda b, i, j: (0, j)),
                  pl.BlockSpec((None, tf, d), lambda b, i, j: (layer, j, 0)),
                  pl.BlockSpec((1, d), lambda b, i, j: (0, 0))],
        out_specs=pl.BlockSpec((1, tm, d), lambda b, i, j: (b, i, 0)),
        out_shape=jax.ShapeDtypeStruct((bm, t, d), F32),
        scratch_shapes=[pltpu.VMEM((tm + 2 * HALO, d), BF16)],
        compiler_params=_cparams("parallel", "parallel", "arbitrary"),
        name=name,
    )(h, h, h, g.reshape(1, d), shift, scale, gate, w_up, w_up, conv_w, conv_b.reshape(1, f), w_down,
      final_g.reshape(1, d))


def _attn_kernel(*refs, has_lat):
    if has_lat:
        (q_ref, kc_ref, vc_ref, kl_ref, vl_ref, qn_ref, kn_ref, cq_ref, sq_ref, ck_ref, sk_ref,
         o_ref, kc_s, kl_s) = refs
    else:
        q_ref, kc_ref, vc_ref, qn_ref, kn_ref, o_ref, kc_s = refs

    @pl.when(pl.program_id(2) == 0)
    def _():
        kc_s[...] = _rms(kc_ref[0].astype(F32), kn_ref[...]).astype(BF16)
        if has_lat:
            kl = _rms(kl_ref[0].astype(F32), kn_ref[...])
            kl_s[...] = _rope(kl, ck_ref[...], sk_ref[...]).astype(BF16)

    scale = HEAD_DIM ** -0.5 * LOG2_E
    for hh in range(ATT_GROUP):
        cols = slice(hh * HEAD_DIM, (hh + 1) * HEAD_DIM)
        q = _rms(q_ref[0, :, cols].astype(F32), qn_ref[...])
        if has_lat:
            q = _rope(q, cq_ref[...], sq_ref[...])
        q = (q * scale).astype(BF16)
        s_c = _dot_nt(q, kc_s[...])
        m = jnp.max(s_c, axis=-1, keepdims=True)
        if has_lat:
            s_l = _dot_nt(q, kl_s[...])
            m = jnp.maximum(m, jnp.max(s_l, axis=-1, keepdims=True))
        p_c = jnp.exp2(s_c - m)
        denom = jnp.sum(p_c, axis=-1, keepdims=True)
        acc = _dot(p_c.astype(BF16), vc_ref[0])
        if has_lat:
            p_l = jnp.exp2(s_l - m)
            denom = denom + jnp.sum(p_l, axis=-1, keepdims=True)
            acc = acc + _dot(p_l.astype(BF16), vl_ref[0])
        o_ref[0, :, cols] = (acc / denom).astype(o_ref.dtype)


def _attention(p_q, p_ctx, p_lat, q_norm, k_norm, cos_t, sin_t):
    b, t, _ = p_q.shape
    n_c = p_ctx.shape[1]
    has_lat = p_lat is not None
    tq = min(t, 256)
    gw = ATT_GROUP * HEAD_DIM
    q_blk, k_blk, v_blk = 2048 // gw, (2048 + 1024) // HEAD_DIM, (2048 + 1024 + 256) // HEAD_DIM
    vec = pl.BlockSpec((1, HEAD_DIM), lambda bi, g, qi: (0, 0))
    in_specs = [pl.BlockSpec((1, tq, gw), lambda bi, g, qi: (bi, qi, q_blk + g)),
                pl.BlockSpec((1, n_c, HEAD_DIM), lambda bi, g, qi: (bi, 0, k_blk + g)),
                pl.BlockSpec((1, n_c, HEAD_DIM), lambda bi, g, qi: (bi, 0, v_blk + g))]
    args = [p_q, p_ctx, p_ctx]
    scratch = [pltpu.VMEM((n_c, HEAD_DIM), BF16)]
    if has_lat:
        n_l = p_lat.shape[1]
        in_specs += [pl.BlockSpec((1, n_l, HEAD_DIM), lambda bi, g, qi: (bi, 0, k_blk + g)),
                     pl.BlockSpec((1, n_l, HEAD_DIM), lambda bi, g, qi: (bi, 0, v_blk + g))]
        args += [p_lat, p_lat]
        scratch.append(pltpu.VMEM((n_l, HEAD_DIM), BF16))
    in_specs += [vec, vec]
    args += [q_norm.reshape(1, HEAD_DIM), k_norm.reshape(1, HEAD_DIM)]
    if has_lat:
        in_specs += [pl.BlockSpec((tq, HEAD_DIM), lambda bi, g, qi: (qi, 0)),
                     pl.BlockSpec((tq, HEAD_DIM), lambda bi, g, qi: (qi, 0)),
                     pl.BlockSpec((n_l, HEAD_DIM), lambda bi, g, qi: (0, 0)),
                     pl.BlockSpec((n_l, HEAD_DIM), lambda bi, g, qi: (0, 0))]
        args += [cos_t, sin_t, cos_t, sin_t]
    return pl.pallas_call(
        functools.partial(_attn_kernel, has_lat=has_lat),
        grid=(b, ATT_HEADS // ATT_GROUP, t // tq),
        in_specs=in_specs,
        out_specs=pl.BlockSpec((1, tq, gw), lambda bi, g, qi: (bi, qi, g)),
        out_shape=jax.ShapeDtypeStruct((b, t, ATT_HEADS * HEAD_DIM), BF16),
        scratch_shapes=scratch,
        compiler_params=_cparams("parallel", "parallel", "arbitrary"),
        name="attention_lat" if has_lat else "attention_ctx",
    )(*args)


def _conformer_kernel(x_ref, xp_ref, xn_ref, w_ref, b_ref, g_ref, be_ref, o_ref, xs, *, tt):
    i, n = pl.program_id(1), pl.num_programs(1)

    def glu(ref):
        v = ref[0].astype(F32)
        return v[:, :CONV_WIDTH] * jax.nn.sigmoid(v[:, CONV_WIDTH:])

    xs[0:HALO] = jnp.where(i > 0, glu(xp_ref), 0.0)
    xs[HALO:HALO + tt] = glu(x_ref)
    xs[HALO + tt:] = jnp.where(i < n - 1, glu(xn_ref), 0.0)
    pad = CONV_KERNEL // 2
    offs = [HALO - pad + k for k in range(CONV_KERNEL)]
    window = xs[...]
    rows = tt + 2 * HALO
    acc = None
    for phase in range(SUBLANES):
        group = [o for o in offs if o % SUBLANES == phase]
        if not group:
            continue
        base = window if phase == 0 else pltpu.roll(window, rows - phase, 0)
        for o in group:
            k = o - (HALO - pad)
            term = w_ref[k:k + 1, :] * base[o - phase:o - phase + tt, :]
            acc = term if acc is None else acc + term
    h = acc + b_ref[...]
    hc = h - jnp.mean(h, axis=-1, keepdims=True)
    var = jnp.mean(hc * hc, axis=-1, keepdims=True)
    y = hc * lax.rsqrt(var + NORM_EPS) * g_ref[...] + be_ref[...]
    o_ref[0] = _silu(y).astype(o_ref.dtype)


def _conformer(p, conv_w, conv_b, norm_g, norm_b):
    b, t, _ = p.shape
    tt = min(t, 512)
    nh = t // HALO
    cw = CONV_WIDTH
    vec = pl.BlockSpec((1, cw), lambda bi, i: (0, 0))
    return pl.pallas_call(
        functools.partial(_conformer_kernel, tt=tt),
        grid=(b, t // tt),
        in_specs=[pl.BlockSpec((1, tt, 2 * cw), lambda bi, i: (bi, i, 0)),
                  pl.BlockSpec((1, HALO, 2 * cw), lambda bi, i: (bi, jnp.maximum(i * (tt // HALO) - 1, 0), 0)),
                  pl.BlockSpec((1, HALO, 2 * cw), lambda bi, i: (bi, jnp.minimum((i + 1) * (tt // HALO), nh - 1), 0)),
                  pl.BlockSpec((CONV_KERNEL, cw), lambda bi, i: (0, 0)),
                  vec, vec, vec],
        out_specs=pl.BlockSpec((1, tt, cw), lambda bi, i: (bi, i, 0)),
        out_shape=jax.ShapeDtypeStruct((b, t, cw), BF16),
        scratch_shapes=[pltpu.VMEM((tt + 2 * HALO, cw), F32)],
        compiler_params=_cparams("parallel", "parallel"),
        name="conformer",
    )(p, p, p, conv_w, conv_b.reshape(1, cw), norm_g.reshape(1, cw), norm_b.reshape(1, cw))


def _head_ones(n, width):
    r = lax.broadcasted_iota(jnp.int32, (n, n), 0) // width
    c = lax.broadcasted_iota(jnp.int32, (n, n), 1) // width
    return jnp.where(r == c, 1.0, 0.0).astype(BF16)


def _softplus(x):
    return jnp.maximum(x, 0.0) + jnp.log(1.0 + jnp.exp(-jnp.abs(x)))


def _rwkv_feat_kernel(z_ref, zp_ref, zn_ref, mu_ref, w0_ref, w2_ref, a0_ref, a2_ref, g2_ref, kk_ref, ka_ref,
                      r_o, v_o, kk_o, lw0_o, lw1_o, kd0_o, kd1_o, b0_o, b1_o, gate_o, zs, *, tt):
    i, n = pl.program_id(1), pl.num_programs(1)
    zs[0:HALO] = jnp.where(i > 0, zp_ref[0].astype(F32), 0.0)
    zs[HALO:HALO + tt] = z_ref[0].astype(F32)
    zs[HALO + tt:] = jnp.where(i < n - 1, zn_ref[0].astype(F32), 0.0)
    window = zs[...]
    rows = tt + 2 * HALO
    z = window[HALO:HALO + tt]
    z_prev = pltpu.roll(window, 1, 0)[HALO:HALO + tt]
    z_next = pltpu.roll(window, rows - 1, 0)[HALO:HALO + tt]
    x = z + mu_ref[0:1, :] * (z_prev - z) + mu_ref[1:2, :] * (z_next - z)
    w = RWKV_WIDTH
    r, k, v = x[:, :w], x[:, w:2 * w], x[:, 2 * w:3 * w]
    lora = x[:, 3 * w:3 * w + LORA_IN]
    gate_in = x[:, 3 * w + LORA_IN:]
    r_o[0] = r.astype(r_o.dtype)
    v_o[0] = v.astype(v_o.dtype)
    kk = k * kk_ref[...]
    ones = _head_ones(LANES, RWKV_HEAD)
    sq = kk * kk
    ssum = jnp.concatenate([_split_dot(sq[:, c * LANES:(c + 1) * LANES], ones, 2) for c in range(w // LANES)], axis=-1)
    kk = kk * lax.rsqrt(ssum + 1e-12)
    kk_o[0] = kk.astype(kk_o.dtype)
    lora_t = jnp.tanh(lora).astype(BF16)
    lora_b = lora.astype(BF16)
    for d, (lw_o, kd_o, b_o) in enumerate(((lw0_o, kd0_o, b0_o), (lw1_o, kd1_o, b1_o))):
        wl = -_softplus(-(w0_ref[d:d + 1, :] + _dot(lora_t, w2_ref[d]))) - 0.5
        lw_o[0] = -jnp.exp(wl)
        a = jax.nn.sigmoid(a0_ref[d:d + 1, :] + _dot(lora_b, a2_ref[d]))
        kd_o[0] = (k * (1.0 + (a - 1.0) * ka_ref[...])).astype(kd_o.dtype)
        b_o[0] = (kk * a).astype(b_o.dtype)
    gate_o[0] = _dot(jax.nn.sigmoid(gate_in).astype(BF16), g2_ref[...]).astype(gate_o.dtype)


def _rwkv_features(z, mu, w0, w2p, a0, a2p, g2p, k_k, k_a):
    b, t, zw = z.shape
    tt = min(t, 256)
    nh = t // HALO
    w = RWKV_WIDTH
    vec = pl.BlockSpec((1, w), lambda bi, i: (0, 0))
    out_spec = pl.BlockSpec((1, tt, w), lambda bi, i: (bi, i, 0))
    outs = [jax.ShapeDtypeStruct((b, t, w), F32 if k in (3, 4) else BF16) for k in range(10)]
    return pl.pallas_call(
        functools.partial(_rwkv_feat_kernel, tt=tt),
        grid=(b, t // tt),
        in_specs=[pl.BlockSpec((1, tt, zw), lambda bi, i: (bi, i, 0)),
                  pl.BlockSpec((1, HALO, zw), lambda bi, i: (bi, jnp.maximum(i * (tt // HALO) - 1, 0), 0)),
                  pl.BlockSpec((1, HALO, zw), lambda bi, i: (bi, jnp.minimum((i + 1) * (tt // HALO), nh - 1), 0)),
                  pl.BlockSpec((2, zw), lambda bi, i: (0, 0)),
                  pl.BlockSpec((2, w), lambda bi, i: (0, 0)),
                  pl.BlockSpec((2, LORA_IN, w), lambda bi, i: (0, 0, 0)),
                  pl.BlockSpec((2, w), lambda bi, i: (0, 0)),
                  pl.BlockSpec((2, LORA_IN, w), lambda bi, i: (0, 0, 0)),
                  pl.BlockSpec((GATE_PAD, w), lambda bi, i: (0, 0)),
                  vec, vec],
        out_specs=[out_spec] * 10,
        out_shape=outs,
        scratch_shapes=[pltpu.VMEM((tt + 2 * HALO, zw), F32)],
        compiler_params=_cparams("parallel", "parallel"),
        name="rwkv_features",
    )(z, z, z, mu, w0, w2p, a0, a2p, g2p, k_k.reshape(1, w), k_a.reshape(1, w))


def _rwkv_chunk_terms(chains, want_y, between=()):
    pending = list(between)

    def tick():
        if pending:
            pending.pop(0)()

    n = len(chains)
    rng = range(n)
    c = chains[0][1].shape[0]
    low_lane = chains[0][6][3]
    block_mask = chains[0][6][4]
    r, v, kk, lw, kd, beta = ([ch[k] for ch in chains] for k in range(6))
    tri_incl, incl2, strict2 = ([ch[6][k] for ch in chains] for k in range(3))

    def stack_heads(x):
        return jnp.concatenate([jnp.where(low_lane, x, 0.0), jnp.where(low_lane, 0.0, x)], axis=0)

    def unstack(x):
        return jnp.where(low_lane, x[:c], x[c:])

    cs = [_ones_dot(tri_incl[i], lw[i], 2) for i in rng]
    tick()
    tot = [jnp.sum(lw[i], axis=0, keepdims=True) for i in rng]
    e_neg = [jnp.exp(-cs[i]) for i in rng]
    e_rem = [jnp.exp(tot[i] - cs[i]) for i in rng]
    kt_st = [stack_heads(kk[i] * jnp.exp(cs[i] - lw[i])) for i in rng]
    kt_sb = [kt_st[i].astype(BF16) for i in rng]
    bh_sb = [stack_heads(beta[i] * e_neg[i]).astype(BF16) for i in rng]
    kh_sb = [stack_heads(kd[i] * e_neg[i]).astype(BF16) for i in rng]
    right = [jnp.concatenate([beta[i] * e_rem[i], kd[i] * e_rem[i]], axis=0).astype(BF16) for i in rng]
    v_b = [v[i].astype(BF16) for i in rng]
    v2 = [jnp.concatenate([v_b[i], v_b[i]], axis=0) for i in rng]
    n_st = 2 * c
    lhs_a = kt_sb
    if want_y:
        rt = [r[i] * jnp.exp(cs[i]) for i in rng]
        lhs_a = [jnp.concatenate([kt_sb[i], stack_heads(rt[i]).astype(BF16)], axis=0) for i in rng]
    aa = [_dot_nt(lhs_a[i], jnp.concatenate([bh_sb[i], kh_sb[i]], axis=0)) for i in rng]
    a_kb = [aa[i][:n_st, :n_st] * strict2[i] for i in rng]
    a_kk = [(aa[i][:n_st, n_st:] * strict2[i]).astype(BF16) for i in rng]
    if want_y:
        a_rb = [(aa[i][n_st:, :n_st] * incl2[i]).astype(BF16) for i in rng]
        a_rk = [(aa[i][n_st:, n_st:] * incl2[i]).astype(BF16) for i in rng]
    tick()
    rhs0 = [_dot(a_kk[i], v2[i]) for i in rng]
    tick()
    x = [-a_kb[i] for i in rng]
    p_b = [x[i].astype(BF16) for i in rng]
    p = [_dot(p_b[i], p_b[i]) for i in rng]
    tick()
    levels = int(np.log2(c))
    for k in range(1, levels):
        p_b = [p[i].astype(BF16) for i in rng]
        if k < levels - 1:
            prod = [_dot(p_b[i], jnp.concatenate([p_b[i], x[i].astype(BF16)], axis=1)) for i in rng]
            x = [x[i] + p[i] + prod[i][:, n_st:] for i in rng]
            p = [prod[i][:, :n_st] for i in rng]
        else:
            x = [x[i] + p[i] + _dot(p_b[i], x[i].astype(BF16)) for i in rng]
        tick()
    both = [_dot(x[i].astype(BF16), jnp.concatenate([rhs0[i].astype(BF16), kt_sb[i]], axis=1)) for i in rng]
    u0 = [unstack(-(rhs0[i] + both[i][:, :LANES])) for i in rng]
    kq_st = [kt_st[i] + both[i][:, LANES:] for i in rng]
    kq = [unstack(kq_st[i]) for i in rng]
    lhs_t = [jnp.concatenate([jnp.concatenate([u0[i], v[i]], axis=0).T,
                              jnp.concatenate([kq[i], jnp.zeros_like(kq[i])], axis=0).T], axis=0).astype(BF16)
             for i in rng]
    tick()
    gq = [_dot(lhs_t[i], right[i]) for i in rng]
    tick()
    if want_y:
        u0_b = [u0[i].astype(BF16) for i in rng]
        rhs_y = [jnp.concatenate(
            [jnp.concatenate([u0_b[i], u0_b[i], v2[i]], axis=0),
             jnp.concatenate([kq_st[i].astype(BF16), jnp.zeros((n_st, LANES), BF16)], axis=0)], axis=1) for i in rng]
        yy = [_dot(jnp.concatenate([a_rb[i], a_rk[i]], axis=1), rhs_y[i]) for i in rng]
        y0 = [yy[i][:, :LANES] for i in rng]
        rq = [yy[i][:, LANES:] for i in rng]
    while pending:
        tick()
    out = []
    for i in rng:
        g_add = gq[i][:LANES] * block_mask
        q_mat = (gq[i][LANES:] * block_mask).astype(BF16)
        if want_y:
            out.append((jnp.exp(tot[i]), q_mat, g_add, (rt[i] - unstack(rq[i])).astype(BF16), unstack(y0[i])))
        else:
            out.append((jnp.exp(tot[i]), q_mat, g_add, None, None))
    return out


def _rwkv_consts(c, reverse):
    t = lax.broadcasted_iota(jnp.int32, (c, c), 0)
    s = lax.broadcasted_iota(jnp.int32, (c, c), 1)
    incl = (s >= t) if reverse else (s <= t)
    tri_incl = jnp.where(incl, 1.0, 0.0).astype(BF16)
    t2 = lax.broadcasted_iota(jnp.int32, (2 * c, 2 * c), 0)
    s2 = lax.broadcasted_iota(jnp.int32, (2 * c, 2 * c), 1)
    same = (t2 // c) == (s2 // c)
    order = (s2 >= t2) if reverse else (s2 <= t2)
    incl2 = jnp.where(same & order, 1.0, 0.0)
    strict2 = jnp.where(same & order & (s2 != t2), 1.0, 0.0)
    low_lane = lax.broadcasted_iota(jnp.int32, (1, LANES), 1) < RWKV_HEAD
    bi = lax.broadcasted_iota(jnp.int32, (LANES, LANES), 0) // RWKV_HEAD
    bj = lax.broadcasted_iota(jnp.int32, (LANES, LANES), 1) // RWKV_HEAD
    block_mask = jnp.where(bi == bj, 1.0, 0.0)
    return tri_incl, incl2, strict2, low_lane, block_mask


def _rwkv_kernel(r_l, v_l, kk_l, lw0_l, lw1_l, kd0_l, kd1_l, b0_l, b1_l, gate_l,
                 v_c, kk_c, lw0_c, lw1_c, kd0_c, kd1_c, b0_c, b1_c,
                 rk_ref, lng_ref, lnb_ref, o_ref, acc_s, gam_s, q_s, g_s, re_s):
    c = RWKV_CHUNK
    n_l, n_c = r_l.shape[1] // c, v_c.shape[1] // c
    consts = (_rwkv_consts(c, False), _rwkv_consts(c, True))
    lat_dirs = ((lw0_l, kd0_l, b0_l), (lw1_l, kd1_l, b1_l))
    ctx_dirs = ((lw0_c, kd0_c, b0_c), (lw1_c, kd1_c, b1_c))
    acc_s[...] = jnp.zeros_like(acc_s)
    state = [jnp.zeros((LANES, LANES), F32), jnp.zeros((LANES, LANES), F32)]

    def chunk_rows(ci):
        return slice(ci * c, (ci + 1) * c)

    def scan_step(slot_f, slot_b, lat_f=None, lat_b=None):
        def step():
            nxt = []
            for d, slot, lat in ((0, slot_f, lat_f), (1, slot_b, lat_b)):
                s_in = state[d]
                s_b = s_in.astype(BF16)
                nxt.append(s_in * gam_s[d, slot] - _dot(s_b, q_s[d, slot]) + g_s[d, slot])
                if lat is not None:
                    acc_s[chunk_rows(lat), :] += _dot_nt(re_s[d, lat], s_b)
            state[:] = nxt
        return step

    def build_terms(groups, refs, dirs, slot0, want_y, between):
        r_ref, v_ref, kk_ref = refs
        chains, where = [], []
        for d, (lw_r, kd_r, b_r) in enumerate(dirs):
            for ci in groups[d]:
                rows = chunk_rows(ci)
                r = r_ref[0, rows, :].astype(F32) if want_y else None
                chains.append((r, v_ref[0, rows, :].astype(F32), kk_ref[0, rows, :].astype(F32), lw_r[0, rows, :],
                               kd_r[0, rows, :].astype(F32), b_r[0, rows, :].astype(F32), consts[d]))
                where.append((d, ci))
        terms = _rwkv_chunk_terms(chains, want_y, between)
        for (d, ci), (gam, q_mat, g_add, r_eff, y0) in zip(where, terms):
            gam_s[d, slot0 + ci] = gam
            q_s[d, slot0 + ci] = q_mat
            g_s[d, slot0 + ci] = g_add
            if want_y:
                re_s[d, ci] = r_eff
                acc_s[chunk_rows(ci), :] += y0

    group = min(RWKV_GROUP, n_l)
    n_groups = n_l // group
    build_terms((range(n_c), range(n_c)), (None, v_c, kk_c), ctx_dirs, 0, False, ())
    steps = [scan_step(k, n_c - 1 - k) for k in range(n_c)]
    for gi in range(n_groups):
        gb = n_groups - 1 - gi
        build_terms((range(gi * group, (gi + 1) * group), range(gb * group, (gb + 1) * group)),
                    (r_l, v_l, kk_l), lat_dirs, n_c, True, steps)
        steps = [scan_step(n_c + gi * group + k, n_c + (gb + 1) * group - 1 - k,
                           gi * group + k, (gb + 1) * group - 1 - k) for k in range(group)]
    for step in steps:
        step()

    ones = _head_ones(LANES, RWKV_HEAD)
    wkv = acc_s[...]
    inv_n = 1.0 / RWKV_HEAD
    mean = _split_dot(wkv, ones, 2) * inv_n
    xc = wkv - mean
    var = _split_dot(xc * xc, ones, 2) * inv_n
    y = xc * lax.rsqrt(var + RWKV_GN_EPS) * lng_ref[...] + lnb_ref[...]
    r = r_l[0].astype(F32)
    rk = rk_ref[...]
    bonus = (_split_dot(r * kd0_l[0].astype(F32) * rk, ones, 2)
             + _split_dot(r * kd1_l[0].astype(F32) * rk, ones, 2)) * v_l[0].astype(F32)
    o_ref[0] = ((y + bonus) * gate_l[0].astype(F32)).astype(o_ref.dtype)


def _rwkv_mix(feat_l, feat_c, r_k, ln_g, ln_b):
    r_l, v_l, kk_l, lw0_l, lw1_l, kd0_l, kd1_l, b0_l, b1_l, gate_l = feat_l
    _, v_c, kk_c, lw0_c, lw1_c, kd0_c, kd1_c, b0_c, b1_c, _ = feat_c
    b, t, w = r_l.shape
    t_c = v_c.shape[1]
    n_slots = (t_c + t) // RWKV_CHUNK
    lat = pl.BlockSpec((1, t, LANES), lambda bi, p: (bi, 0, p))
    ctx = pl.BlockSpec((1, t_c, LANES), lambda bi, p: (bi, 0, p))
    vec = pl.BlockSpec((1, LANES), lambda bi, p: (0, p))
    return pl.pallas_call(
        _rwkv_kernel,
        grid=(b, w // LANES),
        in_specs=[lat] * 10 + [ctx] * 8 + [vec] * 3,
        out_specs=lat,
        out_shape=jax.ShapeDtypeStruct((b, t, w), BF16),
        scratch_shapes=[pltpu.VMEM((t, LANES), F32),
                        pltpu.VMEM((2, n_slots, 1, LANES), F32),
                        pltpu.VMEM((2, n_slots, LANES, LANES), BF16),
                        pltpu.VMEM((2, n_slots, LANES, LANES), F32),
                        pltpu.VMEM((2, t // RWKV_CHUNK, RWKV_CHUNK, LANES), BF16)],
        compiler_params=_cparams("parallel", "parallel"),
        name="rwkv_mix",
    )(r_l, v_l, kk_l, lw0_l, lw1_l, kd0_l, kd1_l, b0_l, b1_l, gate_l,
      v_c, kk_c, lw0_c, lw1_c, kd0_c, kd1_c, b0_c, b1_c,
      r_k.reshape(1, w), ln_g.reshape(1, w), ln_b.reshape(1, w))


def _retention_kernel(dl_ref, q_l, k_l, v_l, rg_l, k_c, v_c, cos_ref, sin_ref, gg_ref, gb_ref, o_ref,
                      q_s, k_s, kv_s, rin_s):
    c = RET_CHUNK
    n_l, n_c = q_l.shape[1] // c, k_c.shape[1] // c
    h = pl.program_id(1)
    k_scale = HEAD_DIM ** -0.5
    q_s[...] = _rope(q_l[0].astype(F32), cos_ref[...], sin_ref[...]).astype(BF16)
    k_s[...] = (_rope(k_l[0].astype(F32), cos_ref[...], sin_ref[...]) * k_scale).astype(BF16)
    ti = lax.broadcasted_iota(jnp.int32, (c, c), 0).astype(F32)
    si = lax.broadcasted_iota(jnp.int32, (c, c), 1).astype(F32)
    col = lax.broadcasted_iota(jnp.int32, (c, 1), 0).astype(F32)
    row = lax.broadcasted_iota(jnp.int32, (1, c), 1).astype(F32)
    lg = [jax.nn.log_sigmoid(jnp.full((1, LANES), dl_ref[d, h], F32))[:, :1] for d in range(2)]
    fwd = jnp.where(ti >= si, jnp.exp(lg[0] * jnp.maximum(ti - si, 0.0)), 0.0)
    bwd = jnp.where(si >= ti, jnp.exp(lg[1] * jnp.maximum(si - ti, 0.0)), 0.0)
    inner = fwd + bwd
    q_decay = (jnp.exp(lg[0] * (col + 1.0)), jnp.exp(lg[1] * (c - col)))
    k_decay = (jnp.exp(lg[0] * (c - 1.0 - row)), jnp.exp(lg[1] * row))
    chunk_decay = (jnp.exp(lg[0] * c), jnp.exp(lg[1] * c))

    def kv_body(ii, carry, *, group, slot0, lat):
        ks, vs, slots = [], [], []
        for g in range(group):
            ci = ii * group + g
            rows = pl.ds(pl.multiple_of(ci * c, c), c)
            ks.append(k_s[rows, :].astype(F32) if lat else k_c[0, rows, :].astype(F32) * k_scale)
            vs.append(v_l[0, rows, :] if lat else v_c[0, rows, :])
            slots.append(slot0 + ci)
        kts = [k.T for k in ks]
        lhs = [jnp.concatenate([kt * k_decay[0], kt * k_decay[1]], axis=0).astype(BF16) for kt in kts]
        kv = [_dot(a, v) for a, v in zip(lhs, vs)]
        for slot, x in zip(slots, kv):
            kv_s[0, slot] = x[:HEAD_DIM]
            kv_s[1, slot] = x[HEAD_DIM:]
        return carry

    g_c = min(RET_GROUP, n_c)
    lax.fori_loop(0, n_c // g_c, functools.partial(kv_body, group=g_c, slot0=0, lat=False), 0)
    lax.fori_loop(0, n_l // RET_GROUP, functools.partial(kv_body, group=RET_GROUP, slot0=n_c, lat=True), 0)

    def scan_body(k, carry, *, n, slot0, record):
        r_f, r_b = carry
        kb = n - 1 - k
        if record:
            rin_s[0, k] = r_f.astype(BF16)
            rin_s[1, kb] = r_b.astype(BF16)
        return r_f * chunk_decay[0] + kv_s[0, slot0 + k], r_b * chunk_decay[1] + kv_s[1, slot0 + kb]

    zero = jnp.zeros((HEAD_DIM, RET_V_DIM), F32)
    carry = lax.fori_loop(0, n_c, functools.partial(scan_body, n=n_c, slot0=0, record=False), (zero, zero))
    lax.fori_loop(0, n_l, functools.partial(scan_body, n=n_l, slot0=n_c, record=True), carry)

    def out_body(ii, carry):
        qs, kbs, vs, rows_l = [], [], [], []
        for g in range(RET_GROUP):
            ci = ii * RET_GROUP + g
            rows = pl.ds(pl.multiple_of(ci * c, c), c)
            rows_l.append((ci, rows))
            qs.append(q_s[rows, :])
            kbs.append(k_s[rows, :])
            vs.append(v_l[0, rows, :])
        s = [(_dot_nt(q, k) * inner).astype(BF16) for q, k in zip(qs, kbs)]
        qd = []
        for q in qs:
            qf = q.astype(F32)
            qd.append(jnp.concatenate([qf * q_decay[0], qf * q_decay[1]], axis=1).astype(BF16))
        r_in = [jnp.concatenate([rin_s[0, ci], rin_s[1, ci]], axis=0) for ci, _ in rows_l]
        o = [_dot(s[g], vs[g]) + _dot(qd[g], r_in[g]) for g in range(RET_GROUP)]
        for (ci, rows), x in zip(rows_l, o):
            xc = x - jnp.mean(x, axis=-1, keepdims=True)
            var = jnp.mean(xc * xc, axis=-1, keepdims=True)
            y = xc * lax.rsqrt(var + NORM_EPS) * gg_ref[...] + gb_ref[...]
            o_ref[0, rows, :] = (y * _silu(rg_l[0, rows, :].astype(F32))).astype(o_ref.dtype)
        return carry

    lax.fori_loop(0, n_l // RET_GROUP, out_body, 0)


def _retention(p_l, p_c, decay_logit, gn_g, gn_b, cos_t, sin_t):
    b, t, _ = p_l.shape
    t_c = p_c.shape[1]
    nh, dk, dv = RET_HEADS, HEAD_DIM, RET_V_DIM
    k_blk, v_blk, g_blk = nh, (2 * nh * dk) // dv, (2 * nh * dk + nh * dv) // dv
    return pl.pallas_call(
        _retention_kernel,
        grid=(b, nh),
        in_specs=[pl.BlockSpec(memory_space=pltpu.SMEM),
                  pl.BlockSpec((1, t, dk), lambda bi, h: (bi, 0, h)),
                  pl.BlockSpec((1, t, dk), lambda bi, h: (bi, 0, k_blk + h)),
                  pl.BlockSpec((1, t, dv), lambda bi, h: (bi, 0, v_blk + h)),
                  pl.BlockSpec((1, t, dv), lambda bi, h: (bi, 0, g_blk + h)),
                  pl.BlockSpec((1, t_c, dk), lambda bi, h: (bi, 0, k_blk + h)),
                  pl.BlockSpec((1, t_c, dv), lambda bi, h: (bi, 0, v_blk + h)),
                  pl.BlockSpec((t, dk), lambda bi, h: (0, 0)),
                  pl.BlockSpec((t, dk), lambda bi, h: (0, 0)),
                  pl.BlockSpec((1, dv), lambda bi, h: (0, h)),
                  pl.BlockSpec((1, dv), lambda bi, h: (0, h))],
        out_specs=pl.BlockSpec((1, t, dv), lambda bi, h: (bi, 0, h)),
        out_shape=jax.ShapeDtypeStruct((b, t, nh * dv), BF16),
        scratch_shapes=[pltpu.VMEM((t, dk), BF16), pltpu.VMEM((t, dk), BF16),
                        pltpu.VMEM((2, (t_c + t) // RET_CHUNK, dk, dv), F32),
                        pltpu.VMEM((2, t // RET_CHUNK, dk, dv), BF16)],
        compiler_params=_cparams("parallel", "parallel"),
        name="retention",
    )(decay_logit, p_l, p_l, p_l, p_l, p_c, p_c, cos_t, sin_t, gn_g.reshape(1, nh * dv), gn_b.reshape(1, nh * dv))


def kernel(x, c, ctx, c_ctx, mod_w, mod_b, norm_mix_g, norm_ffn_g, ffn_w_up, ffn_conv_w, ffn_conv_b, ffn_w_down, ab_w_in, ab_q_norm, ab_k_norm, ab_conv_w, ab_conv_b, ab_conv_norm_g, ab_conv_norm_b, ab_w_out, cd_w_in, cd_shift_mu, rwkv_w0, rwkv_w2, rwkv_a0, rwkv_a2, rwkv_g2, rwkv_k_k, rwkv_k_a, rwkv_r_k, rwkv_ln_g, rwkv_ln_b, ret_decay_logit, ret_gn_g, ret_gn_b, cd_w_out, final_norm_g):
    b, t, d = x.shape
    t_c = ctx.shape[1]
    att_w = ATT_HEADS * HEAD_DIM
    kv_w = att_w // ATT_GROUP

    pad_rows = (-(b + 1)) % 8
    c_all = jnp.concatenate([c, c_ctx[None, :], jnp.zeros((pad_rows, d), F32)], axis=0)
    mod = _modulation(c_all, mod_w, mod_b)

    def mod_vectors(layer):
        lat = [m.reshape(b, 1, d) for m in jnp.split(mod[layer, :b], 6, axis=-1)]
        cx = [m.reshape(1, 1, d) for m in jnp.split(mod[layer, b:b + 1], 6, axis=-1)]
        return lat, cx

    cos_t, sin_t = _rope_tables(t)
    h = x
    hc = ctx.reshape(1, b * t_c, d)

    (sh1, sc1, g1, sh2, sc2, g2), (csh1, csc1, cg1, csh2, csc2, cg2) = mod_vectors(0)
    w_in = ab_w_in[0]
    glu0 = att_w + 2 * kv_w
    w_in = jnp.concatenate([w_in[:, glu0:], w_in[:, :glu0]], axis=1).astype(BF16)
    p_l = _norm_mod_matmul(h, norm_mix_g[0], sh1, sc1, w_in, BF16, "ab_in_lat")
    p_c = _norm_mod_matmul(hc, norm_mix_g[0], csh1, csc1, w_in, BF16, "ab_in_ctx").reshape(b, t_c, -1)
    att_l = _attention(p_l, p_c, p_l, ab_q_norm[0], ab_k_norm[0], cos_t, sin_t)
    att_c = _attention(p_c, p_c, None, ab_q_norm[0], ab_k_norm[0], None, None)
    cv_l = _conformer(p_l, ab_conv_w[0], ab_conv_b[0], ab_conv_norm_g[0], ab_conv_norm_b[0])
    cv_c = _conformer(p_c, ab_conv_w[0], ab_conv_b[0], ab_conv_norm_g[0], ab_conv_norm_b[0])
    w_out = ab_w_out[0].astype(BF16)
    w_o = [w_out[:att_w], w_out[att_w:]]
    h = _matmul_gated_residual([att_l, cv_l], w_o, h, g1, "ab_out_lat")
    hc = _matmul_gated_residual([att_c.reshape(1, b * t_c, -1), cv_c.reshape(1, b * t_c, -1)], w_o, hc, cg1,
                                "ab_out_ctx")
    w_up, w_down = ffn_w_up.astype(BF16), ffn_w_down.astype(BF16)
    h = _conv_ffn(h, norm_ffn_g[0], sh2, sc2, g2, w_up, ffn_conv_w[0], ffn_conv_b[0], w_down, 0, final_norm_g,
                  t, False, "ffn0_lat")
    hc = _conv_ffn(hc, norm_ffn_g[0], csh2, csc2, cg2, w_up, ffn_conv_w[0], ffn_conv_b[0], w_down, 0,
                   final_norm_g, t_c, False, "ffn0_ctx")

    (sh1, sc1, g1, sh2, sc2, g2), (csh1, csc1, _, _, _, _) = mod_vectors(1)
    rw_in = 3 * RWKV_WIDTH + 2 * RWKV_HEAD + GATE_LORA
    zpad = RWKV_IN_PAD - rw_in
    w_z = jnp.pad(cd_w_in[0][:, :rw_in], ((0, 0), (0, zpad))).astype(BF16)
    w_ret = cd_w_in[0][:, rw_in:].astype(BF16)
    z_l = _norm_mod_matmul(h, norm_mix_g[1], sh1, sc1, w_z, BF16, "cd_in_z_lat")
    z_c = _norm_mod_matmul(hc, norm_mix_g[1], csh1, csc1, w_z, BF16, "cd_in_z_ctx").reshape(b, t_c, -1)
    rp_l = _norm_mod_matmul(h, norm_mix_g[1], sh1, sc1, w_ret, BF16, "cd_in_ret_lat")
    rp_c = _norm_mod_matmul(hc, norm_mix_g[1], csh1, csc1, w_ret, BF16, "cd_in_ret_ctx").reshape(b, t_c, -1)

    mu = jnp.pad(cd_shift_mu[0], ((0, 0), (0, zpad)))
    lora_pad = LORA_IN - RWKV_HEAD
    w2p = jnp.pad(rwkv_w2[0], ((0, 0), (0, lora_pad), (0, 0))).astype(BF16)
    a2p = jnp.pad(rwkv_a2[0], ((0, 0), (lora_pad, 0), (0, 0))).astype(BF16)
    g2p = jnp.pad(rwkv_g2[0], ((0, GATE_PAD - GATE_LORA), (0, 0))).astype(BF16)
    feat_l = _rwkv_features(z_l, mu, rwkv_w0[0], w2p, rwkv_a0[0], a2p, g2p, rwkv_k_k[0], rwkv_k_a[0])
    feat_c = _rwkv_features(z_c, mu, rwkv_w0[0], w2p, rwkv_a0[0], a2p, g2p, rwkv_k_k[0], rwkv_k_a[0])
    y_c = _rwkv_mix(feat_l, feat_c, rwkv_r_k[0], rwkv_ln_g[0], rwkv_ln_b[0])
    y_d = _retention(rp_l, rp_c, ret_decay_logit[0], ret_gn_g[0], ret_gn_b[0], cos_t, sin_t)
    w_out = cd_w_out[0].astype(BF16)
    h = _matmul_gated_residual([y_c, y_d], [w_out[:RWKV_WIDTH], w_out[RWKV_WIDTH:]], h, g1, "cd_out_lat")
    return _conv_ffn(h, norm_ffn_g[1], sh2, sc2, g2, w_up, ffn_conv_w[1], ffn_conv_b[1], w_down, 1, final_norm_g,
                     t, True, "ffn1_lat")
```
